```python
import math
import jax, jax.numpy as jnp
from jax import lax
import numpy as np

D_MODEL = 1024
BATCH = 16
SEQ = 2048
DEPTH = 1

D_MIX = D_MODEL
HEAD_DIM = 64
NSA_HEADS = 8
NSA_KV_HEADS = 2
NSA_WIDTH = NSA_HEADS * HEAD_DIM
KV_W = NSA_KV_HEADS * HEAD_DIM
CONV_GROUPS = 8
CONV_WIDTH = D_MIX - NSA_WIDTH
CONV_K = 3
CMP_BLK = 32
CMP_STRIDE = 16
SEL_BLK = 64
SEL_TOPN = 8
WINDOW = 512
Q_BLK = 64
FORCE_SCORE = 1e4
REL_BUCKETS = 32
REL_MAX_DIST = 128
PEER_HEADS = 8
PEER_NKEYS = 128
PEER_EXPERTS = PEER_NKEYS * PEER_NKEYS
PEER_QDIM = 128
PEER_TOPK = 16
PEER_CHUNK = 128
EPS = 1e-6

IN_SIZES = (NSA_WIDTH, KV_W, KV_W, KV_W, KV_W, KV_W, KV_W, NSA_HEADS * 3,
            CONV_WIDTH, CONV_WIDTH, CONV_WIDTH)
IN_COLS = sum(IN_SIZES)
SPLIT_POINTS = tuple(int(v) for v in np.cumsum(IN_SIZES)[:-1])

kernel_name = 'hybrid_nsa_shortconv_peer'


def rmsnorm(x, g):
    x32 = x.astype(jnp.float32)
    y = x32 * lax.rsqrt(jnp.mean(x32 * x32, axis=-1, keepdims=True) + EPS)
    return y.astype(x.dtype) * g


def masked_softmax(logits, mask):
    logits = jnp.where(mask, logits.astype(jnp.float32), -1e30)
    p = jax.nn.softmax(logits, axis=-1)
    return jnp.where(jnp.any(mask, axis=-1, keepdims=True), p, 0.0)


def rel_bucket(dist):
    n = jnp.maximum(dist, 0)
    exact = REL_BUCKETS // 2
    log_ratio = jnp.log(jnp.maximum(n, 1).astype(jnp.float32) / exact) / math.log(REL_MAX_DIST / exact)
    large = exact + (log_ratio * (REL_BUCKETS - exact)).astype(jnp.int32)
    return jnp.where(n < exact, n, jnp.minimum(large, REL_BUCKETS - 1))


def nsa_mixer(q, kc_tok, vc_tok, ks_tok, vs_tok, kw_tok, vw_tok, gates,
              cmp_pe_k, cmp_pe_v, cmp_wk1, cmp_wk2, cmp_wv1, cmp_wv2, rel_table):
    B, S = q.shape[0], q.shape[1]
    G, R, dh = NSA_KV_HEADS, NSA_HEADS // NSA_KV_HEADS, HEAD_DIM

    n_cmp = (S - CMP_BLK) // CMP_STRIDE + 1
    cmp_start = np.arange(n_cmp) * CMP_STRIDE
    tok_idx = cmp_start[:, None] + np.arange(CMP_BLK)[None, :]
    cmp_end = jnp.asarray(cmp_start + CMP_BLK - 1, jnp.int32)

    def compress(tok, pe, w1, w2):
        blk = tok[:, tok_idx] + pe[None, None, :, None, :]
        flat = blk.transpose(0, 1, 3, 2, 4).reshape(B, n_cmp, G, CMP_BLK * dh)
        return jax.nn.gelu(flat @ w1) @ w2

    kc = compress(kc_tok, cmp_pe_k, cmp_wk1, cmp_wk2)
    vc = compress(vc_tok, cmp_pe_v, cmp_wv1, cmp_wv2)

    n_slc = S // SEL_BLK
    n_sel = min(SEL_TOPN, n_slc)
    slc_start = np.arange(n_slc) * SEL_BLK
    overlap = (cmp_start[:, None] < slc_start[None, :] + SEL_BLK) & (cmp_start[:, None] + CMP_BLK > slc_start[None, :])
    cmp_to_slc = jnp.asarray(overlap, jnp.float32)
    ks_blk = ks_tok.reshape(B, n_slc, SEL_BLK, G, dh).transpose(0, 3, 1, 2, 4)
    vs_blk = vs_tok.reshape(B, n_slc, SEL_BLK, G, dh).transpose(0, 3, 1, 2, 4)

    kw_pad = jnp.pad(kw_tok, ((0, 0), (WINDOW, 0), (0, 0), (0, 0)))
    vw_pad = jnp.pad(vw_tok, ((0, 0), (WINDOW, 0), (0, 0), (0, 0)))

    b_ix = jnp.arange(B)[:, None, None]
    g_ix = jnp.arange(G)[None, :, None]
    rel_g = rel_table.reshape(REL_BUCKETS, G, R)
    blk_ids = jnp.arange(n_slc)
    scale = HEAD_DIM ** -0.5

    def head_bias(dist):
        return jnp.transpose(rel_table[rel_bucket(dist)], (2, 0, 1)).reshape(G, R, *dist.shape)

    def query_block(i):
        t0 = i * Q_BLK
        t = t0 + jnp.arange(Q_BLK)
        qb = lax.dynamic_slice_in_dim(q, t0, Q_BLK, axis=1).reshape(B, Q_BLK, G, R, dh) * scale
        gb = jax.nn.sigmoid(lax.dynamic_slice_in_dim(gates, t0, Q_BLK, axis=1).astype(jnp.float32))
        gb = gb.reshape(B, Q_BLK, G, R, 3)

        s = jnp.einsum('bqgrd,bngd->bgrqn', qb, kc) + head_bias(t[:, None] - cmp_end[None, :])
        p_cmp = masked_softmax(s, cmp_end[None, :] <= t[:, None])
        o_cmp = jnp.einsum('bgrqn,bngd->bqgrd', p_cmp, vc)

        imp = jnp.einsum('bgrqn,nj->bgqj', p_cmp, cmp_to_slc)
        cur = t // SEL_BLK
        forced = (blk_ids[None, :] == 0) | (blk_ids[None, :] == cur[:, None]) | (blk_ids[None, :] == cur[:, None] - 1)
        allowed = blk_ids[None, :] * SEL_BLK <= t[:, None]
        imp = jnp.where(forced, FORCE_SCORE, jnp.where(allowed, imp, -jnp.inf))
        _, sel = lax.top_k(imp, n_sel)
        L = n_sel * SEL_BLK
        sel_flat = sel.reshape(B, G, Q_BLK * n_sel)
        ks = ks_blk[b_ix, g_ix, sel_flat].reshape(B, G, Q_BLK, L, dh)
        vs = vs_blk[b_ix, g_ix, sel_flat].reshape(B, G, Q_BLK, L, dh)
        kpos = (sel[..., None] * SEL_BLK + jnp.arange(SEL_BLK)).reshape(B, G, Q_BLK, L)
        dist = t[:, None] - kpos
        bias = rel_g[rel_bucket(dist), jnp.arange(G)[:, None, None]]
        s = jnp.einsum('bqgrd,bgqld->bgrql', qb, ks) + bias.transpose(0, 1, 4, 2, 3)
        p = masked_softmax(s, (dist >= 0)[:, :, None])
        o_slc = jnp.einsum('bgrql,bgqld->bqgrd', p, vs)

        kw = lax.dynamic_slice_in_dim(kw_pad, t0, Q_BLK + WINDOW, axis=1)
        vw = lax.dynamic_slice_in_dim(vw_pad, t0, Q_BLK + WINDOW, axis=1)
        kpos_w = t0 - WINDOW + jnp.arange(Q_BLK + WINDOW)
        dist_w = t[:, None] - kpos_w[None, :]
        mask_w = (dist_w >= 0) & (dist_w < WINDOW) & (kpos_w[None, :] >= 0)
        s = jnp.einsum('bqgrd,blgd->bgrql', qb, kw) + head_bias(dist_w)
        p = masked_softmax(s, mask_w)
        o_win = jnp.einsum('bgrql,blgd->bqgrd', p, vw)

        o = gb[..., 0:1] * o_cmp + gb[..., 1:2] * o_slc + gb[..., 2:3] * o_win
        return o.reshape(B, Q_BLK, NSA_WIDTH)

    out = lax.map(query_block, jnp.arange(S // Q_BLK))
    return out.transpose(1, 0, 2, 3).reshape(B, S, NSA_WIDTH)


def short_conv_mixer(h, b_gate, c_gate, conv_w):
    z = c_gate * h
    y = lax.conv_general_dilated(z, conv_w.astype(z.dtype), window_strides=(1,),
                                 padding=((CONV_K - 1, 0),),
                                 dimension_numbers=('NWC', 'WIO', 'NWC'),
                                 feature_group_count=CONV_WIDTH)
    return b_gate * y


def peer_ffn(xn, w_q, sub_keys, expert_u, expert_v):
    B, S, D = xn.shape
    q = (xn @ w_q).reshape(B, S, PEER_HEADS, 2, PEER_QDIM // 2)
    s = jnp.einsum('bshpd,hpkd->bshpk', q, sub_keys)
    v_half, i_half = lax.top_k(s, PEER_TOPK)
    cand = v_half[..., 0, :, None] + v_half[..., 1, None, :]
    top_s, top_c = lax.top_k(cand.reshape(B, S, PEER_HEADS, PEER_TOPK * PEER_TOPK), PEER_TOPK)
    i1 = jnp.take_along_axis(i_half[..., 0, :], top_c // PEER_TOPK, axis=-1)
    i2 = jnp.take_along_axis(i_half[..., 1, :], top_c % PEER_TOPK, axis=-1)
    experts = i1 * PEER_NKEYS + i2
    g = jax.nn.softmax(top_s.astype(jnp.float32), axis=-1)
    E = PEER_HEADS * PEER_TOPK
    n_chunk = (B * S) // PEER_CHUNK
    xs = xn.reshape(n_chunk, PEER_CHUNK, D)
    es = experts.reshape(n_chunk, PEER_CHUNK, E)
    gs = g.reshape(n_chunk, PEER_CHUNK, E)

    def chunk(args):
        xc, ec, gc = args
        u = expert_u[ec]
        act = jax.nn.gelu(jnp.einsum('cd,ced->ce', xc, u).astype(jnp.float32))
        return jnp.einsum('ce,ced->cd', gc * act, expert_v[ec])

    return lax.map(chunk, (xs, es, gs)).reshape(B, S, D)


def setup_inputs(seed: int = 0) -> dict:
    key = jax.random.key(seed)
    ks = jax.random.split(key, 24)
    L, D, dh = DEPTH, D_MODEL, HEAD_DIM

    def nrm(k, shape, s):
        return jax.random.normal(k, shape, jnp.float32) * s

    def gain(k, shape):
        return 1.0 + 0.05 * jax.random.normal(k, shape, jnp.float32)

    return {
        'x': nrm(ks[0], (BATCH, SEQ, D), 1.0),
        'c': nrm(ks[1], (BATCH, D), 1.0),
        'ln_mix_g': gain(ks[2], (L, D)),
        'ln_ffn_g': gain(ks[3], (L, D)),
        'w_mod': nrm(ks[4], (L, D, 6 * D), 0.5 * D ** -0.5),
        'b_mod': nrm(ks[5], (L, 6 * D), 0.02),
        'w_in': nrm(ks[6], (L, D, IN_COLS), D ** -0.5),
        'cmp_pe_k': nrm(ks[7], (L, CMP_BLK, dh), 0.1),
        'cmp_pe_v': nrm(ks[8], (L, CMP_BLK, dh), 0.1),
        'cmp_wk1': nrm(ks[9], (L, CMP_BLK * dh, dh), (CMP_BLK * dh) ** -0.5),
        'cmp_wk2': nrm(ks[10], (L, dh, dh), dh ** -0.5),
        'cmp_wv1': nrm(ks[11], (L, CMP_BLK * dh, dh), (CMP_BLK * dh) ** -0.5),
        'cmp_wv2': nrm(ks[12], (L, dh, dh), dh ** -0.5),
        'conv_w': nrm(ks[13], (L, CONV_K, 1, CONV_WIDTH), CONV_K ** -0.5),
        'norm_attn_g': gain(ks[14], (L, NSA_WIDTH)),
        'norm_conv_g': gain(ks[15], (L, CONV_WIDTH)),
        'w_out': nrm(ks[16], (L, D_MIX, D), D_MIX ** -0.5),
        'peer_wq': nrm(ks[17], (L, D, PEER_HEADS * PEER_QDIM), D ** -0.5),
        'peer_keys': nrm(ks[18], (L, PEER_HEADS, 2, PEER_NKEYS, PEER_QDIM // 2), (PEER_QDIM // 2) ** -0.5),
        'peer_u': nrm(ks[19], (L, PEER_EXPERTS, D), D ** -0.5),
        'peer_v': nrm(ks[20], (L, PEER_EXPERTS, D), PEER_HEADS ** -0.5),
        'rel_table': nrm(ks[21], (REL_BUCKETS, NSA_HEADS), 0.5),
        'ln_final_g': gain(ks[22], (D,)),
    }


def reference(x, c, ln_mix_g, ln_ffn_g, w_mod, b_mod, w_in, cmp_pe_k, cmp_pe_v,
              cmp_wk1, cmp_wk2, cmp_wv1, cmp_wv2, conv_w, norm_attn_g, norm_conv_g,
              w_out, peer_wq, peer_keys, peer_u, peer_v, rel_table, ln_final_g):
    B, S = x.shape[0], x.shape[1]
    c_act = jax.nn.silu(c)
    for l in range(DEPTH):
        mod = (c_act @ w_mod[l] + b_mod[l])[:, None, :]
        sh1, sc1, gt1, sh2, sc2, gt2 = jnp.split(mod, 6, axis=-1)

        h = rmsnorm(x, ln_mix_g[l]) * (1 + sc1) + sh1
        proj = h @ w_in[l]
        q, kc, vc, ksl, vsl, kw, vw, gts, cb, cc, ch = jnp.split(proj, SPLIT_POINTS, axis=-1)
        kv = lambda a: a.reshape(B, S, NSA_KV_HEADS, HEAD_DIM)
        o_attn = nsa_mixer(q.reshape(B, S, NSA_HEADS, HEAD_DIM), kv(kc), kv(vc), kv(ksl), kv(vsl),
                           kv(kw), kv(vw), gts.reshape(B, S, NSA_HEADS, 3),
                           cmp_pe_k[l], cmp_pe_v[l], cmp_wk1[l], cmp_wk2[l], cmp_wv1[l], cmp_wv2[l],
                           rel_table)
        o_conv = short_conv_mixer(ch, cb, cc, conv_w[l])
        mixed = jnp.concatenate([rmsnorm(o_attn, norm_attn_g[l]), rmsnorm(o_conv, norm_conv_g[l])], axis=-1)
        x = x + gt1 * (mixed @ w_out[l])

        h2 = rmsnorm(x, ln_ffn_g[l]) * (1 + sc2) + sh2
        x = x + gt2 * peer_ffn(h2, peer_wq[l], peer_keys[l], peer_u[l], peer_v[l])
    return rmsnorm(x, ln_final_g)
```

```python
import functools
import math

import numpy as np
import jax
import jax.numpy as jnp
from jax import lax
from jax.experimental import pallas as pl
from jax.experimental.pallas import tpu as pltpu

D_MODEL = 1024
HEAD_DIM = 64
NSA_HEADS = 8
NSA_KV_HEADS = 2
NSA_REP = NSA_HEADS // NSA_KV_HEADS
NSA_WIDTH = NSA_HEADS * HEAD_DIM
KV_W = NSA_KV_HEADS * HEAD_DIM
CONV_WIDTH = D_MODEL - NSA_WIDTH
CONV_K = 3
CMP_BLK = 32
CMP_STRIDE = 16
SEL_BLK = 64
SEL_TOPN = 8
WINDOW = 512
FORCE_SCORE = 1e4
REL_BUCKETS = 32
REL_MAX_DIST = 128
PEER_HEADS = 8
PEER_NKEYS = 128
PEER_EXPERTS = PEER_NKEYS * PEER_NKEYS
PEER_QDIM = 128
PEER_TOPK = 16
PEER_E = PEER_HEADS * PEER_TOPK
EPS = 1e-6

LANES = 128
SUBLANES = 8
ROW_CHUNKS = D_MODEL // LANES
VMEM_LIMIT = 56 * 1024 * 1024

TM_PROJ = 512
TM_OUT = 256
TQ = 256
TK = 256
TT = 64
NEG = -1e30
HIGHEST = lax.Precision.HIGHEST

C_Q = 0
C_KV = C_Q + NSA_WIDTH
C_GATE = C_KV + 6 * KV_W
C_CONV = C_GATE + LANES
N_COLS = C_CONV + 3 * CONV_WIDTH


def _params(sem):
    return pltpu.CompilerParams(dimension_semantics=sem, vmem_limit_bytes=VMEM_LIMIT)


def _dot(a, b, precision=None):
    return jnp.dot(a, b, preferred_element_type=jnp.float32, precision=precision)


def _dot_nt(a, b, precision=None):
    return lax.dot_general(a, b, (((1,), (1,)), ((), ())),
                           preferred_element_type=jnp.float32, precision=precision)


def _rms(x, n):
    return x * lax.rsqrt(jnp.sum(x * x, axis=-1, keepdims=True) * (1.0 / n) + EPS)


def _mod_kernel(c_ref, w_ref, b_ref, o_ref):
    c = c_ref[...]
    act = c * jax.nn.sigmoid(c)
    o_ref[...] = _dot(act, w_ref[...], HIGHEST) + b_ref[...]


def _modulation(c, w_mod, b_mod):
    B = c.shape[0]
    n = w_mod.shape[1]
    bn = D_MODEL
    return pl.pallas_call(
        _mod_kernel,
        grid=(n // bn,),
        in_specs=[pl.BlockSpec((B, D_MODEL), lambda j: (0, 0)),
                  pl.BlockSpec((D_MODEL, bn), lambda j: (0, j)),
                  pl.BlockSpec((1, bn), lambda j: (0, j))],
        out_specs=pl.BlockSpec((B, bn), lambda j: (0, j)),
        out_shape=jax.ShapeDtypeStruct((B, n), jnp.float32),
        compiler_params=_params(("arbitrary",)),
        name="modulation",
    )(c, w_mod, b_mod.reshape(1, n))


def _inproj_kernel(tiles_per_seq, x_ref, mod_ref, g_ref, w_ref, cw_ref, cg_ref,
                   q_ref, kc_ref, vc_ref, ks_ref, vs_ref, kw_ref, vw_ref, gate_ref, conv_ref,
                   carry_ref):
    i = pl.program_id(0)
    x = x_ref[...]
    sh = mod_ref[0, 0:1, :]
    sc = mod_ref[0, 1:2, :]
    h = (_rms(x, D_MODEL) * g_ref[...]) * (1.0 + sc) + sh
    proj = _dot(h.astype(jnp.bfloat16), w_ref[...])

    for r in range(NSA_REP):
        q_ref[r] = (proj[:, C_Q + r * LANES:C_Q + (r + 1) * LANES] * (HEAD_DIM ** -0.5)
                    ).astype(jnp.bfloat16)
    kv = lambda k: proj[:, C_KV + k * KV_W:C_KV + (k + 1) * KV_W]
    kc_ref[...] = kv(0)
    vc_ref[...] = kv(1)
    ks_ref[...] = kv(2).astype(jnp.bfloat16)
    vs_ref[...] = kv(3).astype(jnp.bfloat16)
    kw_ref[...] = kv(4).astype(jnp.bfloat16)
    vw_ref[...] = kv(5).astype(jnp.bfloat16)
    gate_ref[...] = jax.nn.sigmoid(proj[:, C_GATE:C_GATE + LANES])

    cb = proj[:, C_CONV:C_CONV + CONV_WIDTH]
    cc = proj[:, C_CONV + CONV_WIDTH:C_CONV + 2 * CONV_WIDTH]
    ch = proj[:, C_CONV + 2 * CONV_WIDTH:C_CONV + 3 * CONV_WIDTH]
    z = cc * ch
    tm = z.shape[0]

    @pl.when(i % tiles_per_seq == 0)
    def _():
        carry_ref[...] = jnp.zeros_like(carry_ref)

    prev1 = carry_ref[SUBLANES - 1:SUBLANES, :]
    prev2 = carry_ref[SUBLANES - 2:SUBLANES - 1, :]
    row = lax.broadcasted_iota(jnp.int32, (tm, CONV_WIDTH), 0)
    z1 = jnp.where(row == 0, prev1, pltpu.roll(z, 1, 0))
    z2 = pltpu.roll(z, 2, 0)
    z2 = jnp.where(row == 0, prev2, jnp.where(row == 1, prev1, z2))
    carry_ref[...] = z[tm - SUBLANES:, :]
    y = cw_ref[0:1, :] * z2 + cw_ref[1:2, :] * z1 + cw_ref[2:3, :] * z
    conv_ref[...] = (_rms(cb * y, CONV_WIDTH) * cg_ref[...]).astype(jnp.bfloat16)


def _in_projection(x2, mod3, ln_g, w_perm, conv_w, conv_g, seq):
    n = x2.shape[0]
    tm = TM_PROJ
    tps = seq // tm
    row = lambda i: (i, 0)
    kv_f32 = jax.ShapeDtypeStruct((n, KV_W), jnp.float32)
    kv_bf = jax.ShapeDtypeStruct((n, KV_W), jnp.bfloat16)
    return pl.pallas_call(
        functools.partial(_inproj_kernel, tps),
        grid=(n // tm,),
        in_specs=[pl.BlockSpec((tm, D_MODEL), row),
                  pl.BlockSpec((1, 6, D_MODEL), lambda i: (i // tps, 0, 0)),
                  pl.BlockSpec((1, D_MODEL), lambda i: (0, 0)),
                  pl.BlockSpec((D_MODEL, N_COLS), lambda i: (0, 0)),
                  pl.BlockSpec((CONV_K, CONV_WIDTH), lambda i: (0, 0)),
                  pl.BlockSpec((1, CONV_WIDTH), lambda i: (0, 0))],
        out_specs=[pl.BlockSpec((NSA_REP, tm, LANES), lambda i: (0, i, 0))]
                  + [pl.BlockSpec((tm, KV_W), row)] * 6
                  + [pl.BlockSpec((tm, LANES), row), pl.BlockSpec((tm, CONV_WIDTH), row)],
        out_shape=[jax.ShapeDtypeStruct((NSA_REP, n, LANES), jnp.bfloat16),
                   kv_f32, kv_f32, kv_bf, kv_bf, kv_bf, kv_bf,
                   jax.ShapeDtypeStruct((n, LANES), jnp.float32),
                   jax.ShapeDtypeStruct((n, CONV_WIDTH), jnp.bfloat16)],
        scratch_shapes=[pltpu.VMEM((SUBLANES, CONV_WIDTH), jnp.float32)],
        compiler_params=_params(("arbitrary",)),
        name="in_projection",
    )(x2, mod3, ln_g, w_perm, conv_w, conv_g)


def _compress_kernel(tok_ref, pe_ref, w1_ref, w2_ref, o_ref):
    half = (CMP_BLK // 2) * HEAD_DIM
    w1a = w1_ref[0, 0:half, :]
    w1b = w1_ref[0, half:2 * half, :]
    pe_term = _dot(pe_ref[0], w1_ref[0], HIGHEST)
    ncp = tok_ref.shape[3]
    row = lax.broadcasted_iota(jnp.int32, (ncp, HEAD_DIM), 0)
    for g in range(NSA_KV_HEADS):
        t = tok_ref[0, 0, g]
        a = _dot(t, w1a, HIGHEST)
        b = _dot(t, w1b, HIGHEST)
        pre = a + pltpu.roll(b, ncp - 1, 0) + pe_term
        out = _dot(jax.nn.gelu(pre), w2_ref[0], HIGHEST)
        o_ref[0, 0, :, g * HEAD_DIM:(g + 1) * HEAD_DIM] = jnp.where(row < ncp - 1, out, 0.0)


def _compress(tok, pe, w1, w2):
    _, B, G, ncp, width = tok.shape
    return pl.pallas_call(
        _compress_kernel,
        grid=(2, B),
        in_specs=[pl.BlockSpec((1, 1, G, ncp, width), lambda k, b: (k, b, 0, 0, 0)),
                  pl.BlockSpec((1, 1, CMP_BLK * HEAD_DIM), lambda k, b: (k, 0, 0)),
                  pl.BlockSpec((1, CMP_BLK * HEAD_DIM, HEAD_DIM), lambda k, b: (k, 0, 0)),
                  pl.BlockSpec((1, HEAD_DIM, HEAD_DIM), lambda k, b: (k, 0, 0))],
        out_specs=pl.BlockSpec((1, 1, ncp, KV_W), lambda k, b: (k, b, 0, 0)),
        out_shape=jax.ShapeDtypeStruct((2, B, ncp, KV_W), jnp.float32),
        compiler_params=_params(("arbitrary", "arbitrary")),
        name="compress_kv",
    )(tok, pe, w1, w2)


def _softmax_step(s, msk, v, m_old, l_old, acc_old):
    s = jnp.where(msk, s, NEG)
    m_new = jnp.maximum(m_old, jnp.max(s, axis=-1, keepdims=True))
    p = jnp.where(msk, jnp.exp(s - m_new), 0.0)
    alpha = jnp.exp(m_old - m_new)
    l_new = alpha * l_old + jnp.sum(p, axis=-1, keepdims=True)
    acc_new = alpha * acc_old + _dot(p.astype(jnp.bfloat16), v)
    return m_new, l_new, acc_new


def _attn_kernel(n_slc, q_ref, ks_ref, vs_ref, kw_ref, vw_ref, cmp_ref, gate_ref,
                 bcmp_ref, bnear_ref, c2s_ref, ng_ref, o_ref, og0_ref):
    qi = pl.program_id(1)
    t0 = qi * TQ
    rows = NSA_REP * TQ
    ncp = cmp_ref.shape[2]
    q4 = q_ref[...].reshape(rows, LANES)
    lane = lax.broadcasted_iota(jnp.int32, (1, LANES), 1)
    tq_col = t0 + lax.broadcasted_iota(jnp.int32, (TQ, 1), 0)
    t_col = jnp.concatenate([tq_col] * NSA_REP, axis=0)
    rowt = t_col - t0
    colk = lax.broadcasted_iota(jnp.int32, (1, TK), 1)
    gates = gate_ref[...]
    n_near = bnear_ref.shape[1] - 1

    for g in range(NSA_KV_HEADS):
        in_g = (lane // HEAD_DIM) == g
        q = jnp.where(in_g, q4, jnp.zeros_like(q4))

        kc = cmp_ref[0, 0]
        vc = cmp_ref[1, 0]
        s = _dot_nt(q.astype(jnp.float32), kc, HIGHEST) + bcmp_ref[g, 0]
        n_idx = lax.broadcasted_iota(jnp.int32, (1, ncp), 1)
        valid = (n_idx * CMP_STRIDE + (CMP_BLK - 1) <= t_col) & (n_idx < ncp - 1)
        s = jnp.where(valid, s, NEG)
        e = jnp.where(valid, jnp.exp(s - jnp.max(s, axis=-1, keepdims=True)), 0.0)
        den = jnp.sum(e, axis=-1, keepdims=True)
        p_cmp = e / jnp.maximum(den, 1e-30)
        o_cmp = _dot(p_cmp, vc, HIGHEST)

        p_sum = p_cmp[0:TQ]
        for r in range(1, NSA_REP):
            p_sum = p_sum + p_cmp[r * TQ:(r + 1) * TQ]
        imp = _dot(p_sum, c2s_ref[...], HIGHEST)
        lane_f = lane.astype(jnp.float32)
        cur = tq_col // SEL_BLK
        forced = (lane == 0) | (lane == cur) | (lane == cur - 1)
        allowed = lane * SEL_BLK <= tq_col
        score = jnp.where(forced, FORCE_SCORE, jnp.where(allowed, imp, -jnp.inf))
        real = lane < n_slc
        taken = jnp.logical_not(real) | jnp.zeros((TQ, LANES), jnp.bool_)
        for _ in range(min(SEL_TOPN, n_slc)):
            eff = jnp.where(taken, -jnp.inf, score)
            mx = jnp.max(eff, axis=-1, keepdims=True)
            cand = jnp.where((eff == mx) & jnp.logical_not(taken), lane_f, float(LANES))
            first = jnp.min(cand, axis=-1, keepdims=True)
            taken = taken | (lane_f == first)
        sel = jnp.where(taken & real, 1.0, 0.0).astype(jnp.bfloat16)
        blk_row = lax.broadcasted_iota(jnp.int32, (LANES, TK), 0)
        blk_col = lax.broadcasted_iota(jnp.int32, (LANES, TK), 1) // SEL_BLK

        def sel_mask(kt):
            expand = jnp.where(blk_row == blk_col + kt * (TK // SEL_BLK), 1.0, 0.0)
            one = _dot(sel, expand.astype(jnp.bfloat16))
            return jnp.concatenate([one] * NSA_REP, axis=0) > 0.5

        def scores(k_ref, kt):
            k = k_ref[pl.ds(pl.multiple_of(kt * TK, TK), TK), :]
            d = qi - kt
            bias = bnear_ref[g, jnp.minimum(d, n_near)]
            dist = d * TQ + rowt - colk
            return _dot_nt(q, k) + bias, dist

        init = (jnp.full((rows, 1), NEG, jnp.float32), jnp.zeros((rows, 1), jnp.float32),
                jnp.zeros((rows, LANES), jnp.float32))

        def slc_step(kt, carry):
            s, dist = scores(ks_ref, kt)
            v = vs_ref[pl.ds(pl.multiple_of(kt * TK, TK), TK), :]
            return _softmax_step(s, sel_mask(kt) & (dist >= 0), v, *carry)

        def win_step(kt, carry):
            s, dist = scores(kw_ref, kt)
            v = vw_ref[pl.ds(pl.multiple_of(kt * TK, TK), TK), :]
            return _softmax_step(s, (dist >= 0) & (dist < WINDOW), v, *carry)

        _, l_s, acc_s = lax.fori_loop(0, qi + 1, slc_step, init)
        _, l_w, acc_w = lax.fori_loop(jnp.maximum(qi - WINDOW // TK, 0), qi + 1, win_step, init)
        o_slc = acc_s / jnp.maximum(l_s, 1e-30)
        o_win = acc_w / jnp.maximum(l_w, 1e-30)

        for r in range(NSA_REP):
            c0 = (g * NSA_REP + r) * 3
            sl = slice(r * TQ, (r + 1) * TQ)
            o = (gates[:, c0:c0 + 1] * o_cmp[sl] + gates[:, c0 + 1:c0 + 2] * o_slc[sl]
                 + gates[:, c0 + 2:c0 + 3] * o_win[sl])
            if g == 0:
                og0_ref[r] = o
            else:
                og0_ref[r] = jnp.where(in_g, o, og0_ref[r])

    ss = jnp.zeros((TQ, 1), jnp.float32)
    for r in range(NSA_REP):
        o = og0_ref[r]
        ss = ss + jnp.sum(o * o, axis=-1, keepdims=True)
    inv = lax.rsqrt(ss * (1.0 / NSA_WIDTH) + EPS)
    for r in range(NSA_REP):
        o_ref[r] = (og0_ref[r] * inv * ng_ref[r]).astype(jnp.bfloat16)


def _attention(q4, ks, vs, kw, vw, cmp_kv, gates, bias_cmp, bias_near, c2s, norm_g, batch, seq):
    n = batch * seq
    nq = seq // TQ
    ncp = cmp_kv.shape[2]
    n_slc = seq // SEL_BLK
    kv_spec = pl.BlockSpec((seq, KV_W), lambda b, i: (b, 0))
    return pl.pallas_call(
        functools.partial(_attn_kernel, n_slc),
        grid=(batch, nq),
        in_specs=[pl.BlockSpec((NSA_REP, TQ, LANES), lambda b, i: (0, b * nq + i, 0)),
                  kv_spec, kv_spec, kv_spec, kv_spec,
                  pl.BlockSpec((2, 1, ncp, KV_W), lambda b, i: (0, b, 0, 0)),
                  pl.BlockSpec((TQ, LANES), lambda b, i: (b * nq + i, 0)),
                  pl.BlockSpec((NSA_KV_HEADS, 1, NSA_REP * TQ, ncp), lambda b, i: (0, i, 0, 0)),
                  pl.BlockSpec(bias_near.shape, lambda b, i: (0, 0, 0, 0)),
                  pl.BlockSpec(c2s.shape, lambda b, i: (0, 0)),
                  pl.BlockSpec((NSA_REP, 1, LANES), lambda b, i: (0, 0, 0))],
        out_specs=pl.BlockSpec((NSA_REP, TQ, LANES), lambda b, i: (0, b * nq + i, 0)),
        out_shape=jax.ShapeDtypeStruct((NSA_REP, n, LANES), jnp.bfloat16),
        scratch_shapes=[pltpu.VMEM((NSA_REP, TQ, LANES), jnp.float32)],
        compiler_params=_params(("arbitrary", "arbitrary")),
        name="sparse_attention",
    )(q4, ks, vs, kw, vw, cmp_kv, gates, bias_cmp, bias_near, c2s, norm_g)


def _topk_rows(s, k):
    nrow = s.shape[0]
    rowf = lax.broadcasted_iota(jnp.int32, s.shape, 0).astype(jnp.float32)
    slot = lax.broadcasted_iota(jnp.int32, (k, s.shape[1]), 0)
    vals = jnp.zeros((k, s.shape[1]), jnp.float32)
    idxs = jnp.zeros((k, s.shape[1]), jnp.float32)
    for j in range(k):
        m = jnp.max(s, axis=0, keepdims=True)
        first = jnp.min(jnp.where(s == m, rowf, float(nrow)), axis=0, keepdims=True)
        vals = jnp.where(slot == j, m, vals)
        idxs = jnp.where(slot == j, first, idxs)
        s = jnp.where(rowf == first, -jnp.inf, s)
    return vals, idxs


def _route_kernel(attn_ref, conv_ref, x_ref, mod_ref, wout_ref, g_ref, wq_ref, keys_ref,
                  x1_ref, h2_ref, eid_ref, gate_ref):
    mixed = jnp.concatenate([attn_ref[r] for r in range(NSA_REP)] + [conv_ref[...]], axis=-1)
    x1 = x_ref[...] + mod_ref[0, 2:3, :] * _dot(mixed, wout_ref[...])
    x1_ref[...] = x1
    h2 = (_rms(x1, D_MODEL) * g_ref[...]) * (1.0 + mod_ref[0, 4:5, :]) + mod_ref[0, 3:4, :]
    h2_ref[...] = h2
    qp = _dot(h2.astype(jnp.bfloat16), wq_ref[...]).astype(jnp.bfloat16)

    tm = qp.shape[0]
    slot = lax.broadcasted_iota(jnp.int32, (PEER_TOPK, tm), 0)
    for h in range(PEER_HEADS):
        blk = qp[:, h * PEER_QDIM:(h + 1) * PEER_QDIM]
        v1, i1 = _topk_rows(_dot_nt(keys_ref[2 * h], blk), PEER_TOPK)
        v2, i2 = _topk_rows(_dot_nt(keys_ref[2 * h + 1], blk), PEER_TOPK)
        cand = jnp.concatenate([v1[a:a + 1] + v2 for a in range(PEER_TOPK)], axis=0)
        eid = jnp.concatenate([i1[a:a + 1] * float(PEER_NKEYS) + i2 for a in range(PEER_TOPK)],
                              axis=0)
        rowf = lax.broadcasted_iota(jnp.int32, cand.shape, 0).astype(jnp.float32)
        top_s = jnp.zeros((PEER_TOPK, tm), jnp.float32)
        top_e = jnp.zeros((PEER_TOPK, tm), jnp.float32)
        for j in range(PEER_TOPK):
            m = jnp.max(cand, axis=0, keepdims=True)
            first = jnp.min(jnp.where(cand == m, rowf, 1e9), axis=0, keepdims=True)
            hit = rowf == first
            e = jnp.sum(jnp.where(hit, eid, 0.0), axis=0, keepdims=True)
            top_s = jnp.where(slot == j, m, top_s)
            top_e = jnp.where(slot == j, e, top_e)
            cand = jnp.where(hit, -jnp.inf, cand)
        ex = jnp.exp(top_s - jnp.max(top_s, axis=0, keepdims=True))
        gate_ref[h * PEER_TOPK:(h + 1) * PEER_TOPK, :] = ex / jnp.sum(ex, axis=0, keepdims=True)
        eid_ref[h * PEER_TOPK:(h + 1) * PEER_TOPK, :] = top_e.astype(jnp.int32)


def _out_proj_route(attn4, conv_n, x2, mod3, w_out_p, ln_g, wq, keys_x, seq):
    n = x2.shape[0]
    tm = TM_OUT
    tps = seq // tm
    row = lambda i: (i, 0)
    const2 = lambda i: (0, 0)
    return pl.pallas_call(
        _route_kernel,
        grid=(n // tm,),
        in_specs=[pl.BlockSpec((NSA_REP, tm, LANES), lambda i: (0, i, 0)),
                  pl.BlockSpec((tm, CONV_WIDTH), row),
                  pl.BlockSpec((tm, D_MODEL), row),
                  pl.BlockSpec((1, 6, D_MODEL), lambda i: (i // tps, 0, 0)),
                  pl.BlockSpec((D_MODEL, D_MODEL), const2),
                  pl.BlockSpec((1, D_MODEL), const2),
                  pl.BlockSpec((D_MODEL, PEER_HEADS * PEER_QDIM), const2),
                  pl.BlockSpec((2 * PEER_HEADS, PEER_NKEYS, PEER_QDIM), lambda i: (0, 0, 0))],
        out_specs=[pl.BlockSpec((tm, D_MODEL), row), pl.BlockSpec((tm, D_MODEL), row),
                   pl.BlockSpec((PEER_E, tm), lambda i: (0, i)),
                   pl.BlockSpec((PEER_E, tm), lambda i: (0, i))],
        out_shape=[jax.ShapeDtypeStruct((n, D_MODEL), jnp.float32),
                   jax.ShapeDtypeStruct((n, D_MODEL), jnp.float32),
                   jax.ShapeDtypeStruct((PEER_E, n), jnp.int32),
                   jax.ShapeDtypeStruct((PEER_E, n), jnp.float32)],
        compiler_params=_params(("arbitrary",)),
        name="out_proj_route",
    )(attn4, conv_n, x2, mod3, w_out_p, ln_g, wq, keys_x)


ROWS_PER_EXPERT = ROW_CHUNKS // 2


def _fetch_indices(idx_hbm, idx_smem, sem):
    cp = pltpu.make_async_copy(idx_hbm.at[pl.program_id(0)], idx_smem, sem)
    cp.start()
    cp.wait()


def _gather_rows(t, idx_smem, tab_ref, rows_ref):
    for k in range(PEER_E):
        e = idx_smem[t * PEER_E + k]
        rows_ref[k * ROWS_PER_EXPERT:(k + 1) * ROWS_PER_EXPERT, :] = tab_ref[e]


def _split_bf16(x):
    hi = x.astype(jnp.bfloat16).astype(jnp.float32)
    return jnp.concatenate([hi, x - hi], axis=0).astype(jnp.bfloat16)


def _expert_of_row():
    n = lax.broadcasted_iota(jnp.int32, (PEER_E * ROW_CHUNKS, PEER_E), 0) // ROW_CHUNKS
    k = lax.broadcasted_iota(jnp.int32, (PEER_E * ROW_CHUNKS, PEER_E), 1)
    return jnp.where(n == k, 1.0, 0.0)


def _peer_act_kernel(idx_hbm, x_ref, gate_ref, tab_ref, w_ref, idx_smem, sem, rows_ref, z_ref):
    _fetch_indices(idx_hbm, idx_smem, sem)
    m_i = lax.broadcasted_iota(jnp.int32, (2 * ROW_CHUNKS, PEER_E * ROW_CHUNKS), 0) % ROW_CHUNKS
    n_i = lax.broadcasted_iota(jnp.int32, (2 * ROW_CHUNKS, PEER_E * ROW_CHUNKS), 1) % ROW_CHUNKS
    chunk_match = m_i == n_i

    def token(t, _):
        _gather_rows(t, idx_smem, tab_ref, rows_ref)
        x16 = _split_bf16(x_ref[t])
        u = pltpu.bitcast(rows_ref[...], jnp.bfloat16)
        r = _dot_nt(x16, u)
        z_ref[pl.ds(t, 1), :] = jnp.sum(jnp.where(chunk_match, r, 0.0), axis=0, keepdims=True)
        return 0

    lax.fori_loop(0, TT, token, 0)
    act = _dot(z_ref[...], _expert_of_row(), HIGHEST)
    w_ref[...] = gate_ref[...] * jax.nn.gelu(act)


def _peer_out_kernel(idx_hbm, w_ref, tab_ref, o_ref, idx_smem, sem, rows_ref, w8_ref):
    _fetch_indices(idx_hbm, idx_smem, sem)
    w8_ref[...] = _dot_nt(w_ref[...], _expert_of_row(), HIGHEST)
    m_i = lax.broadcasted_iota(jnp.int32, (ROW_CHUNKS, PEER_E * ROW_CHUNKS), 0)
    n_i = lax.broadcasted_iota(jnp.int32, (ROW_CHUNKS, PEER_E * ROW_CHUNKS), 1) % ROW_CHUNKS
    chunk_match = m_i == n_i

    def token(t, _):
        _gather_rows(t, idx_smem, tab_ref, rows_ref)
        lhs = _split_bf16(jnp.where(chunk_match, w8_ref[pl.ds(t, 1), :], 0.0))
        v = pltpu.bitcast(rows_ref[...], jnp.bfloat16)
        o = _dot(lhs, v)
        o_ref[t] = o[0:ROW_CHUNKS] + o[ROW_CHUNKS:2 * ROW_CHUNKS]
        return 0

    lax.fori_loop(0, TT, token, 0)


def _peer_scratch():
    return [pltpu.SMEM((TT * PEER_E,), jnp.int32), pltpu.SemaphoreType.DMA,
            pltpu.VMEM((PEER_E * ROWS_PER_EXPERT, LANES), jnp.int32),
            pltpu.VMEM((TT, PEER_E * ROW_CHUNKS), jnp.float32)]


def _peer_act(idx_tiles, x8, gates, tab_u):
    n = x8.shape[0]
    return pl.pallas_call(
        _peer_act_kernel,
        grid=(n // TT,),
        in_specs=[pl.BlockSpec(memory_space=pl.ANY),
                  pl.BlockSpec((TT, ROW_CHUNKS, LANES), lambda i: (i, 0, 0)),
                  pl.BlockSpec((TT, PEER_E), lambda i: (i, 0)),
                  pl.BlockSpec(tab_u.shape, lambda i: (0, 0, 0))],
        out_specs=pl.BlockSpec((TT, PEER_E), lambda i: (i, 0)),
        out_shape=jax.ShapeDtypeStruct((n, PEER_E), jnp.float32),
        scratch_shapes=_peer_scratch(),
        compiler_params=_params(("arbitrary",)),
        name="peer_activations",
    )(idx_tiles, x8, gates, tab_u)


def _peer_out(idx_tiles, w, tab_v):
    n = w.shape[0]
    return pl.pallas_call(
        _peer_out_kernel,
        grid=(n // TT,),
        in_specs=[pl.BlockSpec(memory_space=pl.ANY),
                  pl.BlockSpec((TT, PEER_E), lambda i: (i, 0)),
                  pl.BlockSpec(tab_v.shape, lambda i: (0, 0, 0))],
        out_specs=pl.BlockSpec((TT, ROW_CHUNKS, LANES), lambda i: (i, 0, 0)),
        out_shape=jax.ShapeDtypeStruct((n, ROW_CHUNKS, LANES), jnp.float32),
        scratch_shapes=_peer_scratch(),
        compiler_params=_params(("arbitrary",)),
        name="peer_combine",
    )(idx_tiles, w, tab_v)


def _pack_table(tab):
    bits = lax.bitcast_convert_type(tab.astype(jnp.bfloat16), jnp.uint16).astype(jnp.uint32)
    bits = bits.reshape(tab.shape[0], ROWS_PER_EXPERT, 2, LANES)
    words = bits[:, :, 0, :] | (bits[:, :, 1, :] << 16)
    return lax.bitcast_convert_type(words, jnp.int32)


def _final_kernel(x1_ref, p_ref, mod_ref, g_ref, o_ref):
    x2 = x1_ref[...] + mod_ref[0, 5:6, :] * p_ref[...]
    o_ref[...] = _rms(x2, D_MODEL) * g_ref[...]


def _final(x1, peer, mod3, g, seq):
    n = x1.shape[0]
    tm = TM_PROJ
    tps = seq // tm
    row = lambda i: (i, 0)
    return pl.pallas_call(
        _final_kernel,
        grid=(n // tm,),
        in_specs=[pl.BlockSpec((tm, D_MODEL), row), pl.BlockSpec((tm, D_MODEL), row),
                  pl.BlockSpec((1, 6, D_MODEL), lambda i: (i // tps, 0, 0)),
                  pl.BlockSpec((1, D_MODEL), lambda i: (0, 0))],
        out_specs=pl.BlockSpec((tm, D_MODEL), row),
        out_shape=jax.ShapeDtypeStruct((n, D_MODEL), jnp.float32),
        compiler_params=_params(("arbitrary",)),
        name="final_norm",
    )(x1, peer, mod3, g)


def _attn_perm():
    idx = []
    for r in range(NSA_REP):
        for g in range(NSA_KV_HEADS):
            h = g * NSA_REP + r
            idx.extend(range(h * HEAD_DIM, (h + 1) * HEAD_DIM))
    return np.asarray(idx, np.int32)


def _rel_bucket_np(dist):
    n = np.maximum(dist, 0)
    exact = REL_BUCKETS // 2
    log_ratio = (np.log(np.maximum(n, 1).astype(np.float32) / np.float32(exact))
                 / np.float32(math.log(REL_MAX_DIST / exact)))
    large = exact + (log_ratio * np.float32(REL_BUCKETS - exact)).astype(np.int32)
    return np.where(n < exact, n, np.minimum(large, REL_BUCKETS - 1)).astype(np.int32)


def _bias_tables(rel_table, seq):
    nq = seq // TQ
    ncp = seq // CMP_STRIDE
    table = rel_table.reshape(REL_BUCKETS, NSA_KV_HEADS, NSA_REP)
    n_near = -(-REL_MAX_DIST // TQ) + 1
    r_i = np.arange(TQ)[:, None]
    c_i = np.arange(TK)[None, :]
    near = np.stack([_rel_bucket_np(d * TQ + r_i - c_i) for d in range(n_near)]
                    + [np.full((TQ, TK), REL_BUCKETS - 1, np.int32)])
    b_near = jnp.transpose(table[near], (3, 0, 4, 1, 2))
    b_near = b_near.reshape(NSA_KV_HEADS, n_near + 1, NSA_REP * TQ, TK)
    t_i = np.arange(seq)[:, None]
    n_i = np.arange(ncp)[None, :]
    cmp_b = _rel_bucket_np(t_i - (n_i * CMP_STRIDE + CMP_BLK - 1))
    b_cmp = jnp.transpose(table[cmp_b], (2, 3, 0, 1))
    b_cmp = b_cmp.reshape(NSA_KV_HEADS, NSA_REP, nq, TQ, ncp).transpose(0, 2, 1, 3, 4)
    b_cmp = b_cmp.reshape(NSA_KV_HEADS, nq, NSA_REP * TQ, ncp)
    return b_cmp, b_near


def _cmp_to_slc(seq):
    ncp = seq // CMP_STRIDE
    n_cmp = (seq - CMP_BLK) // CMP_STRIDE + 1
    n_slc = seq // SEL_BLK
    cs = np.arange(ncp) * CMP_STRIDE
    ss = np.arange(LANES) * SEL_BLK
    ov = (cs[:, None] < ss[None, :] + SEL_BLK) & (cs[:, None] + CMP_BLK > ss[None, :])
    ov &= (np.arange(ncp)[:, None] < n_cmp) & (np.arange(LANES)[None, :] < n_slc)
    return jnp.asarray(ov, jnp.float32)


def kernel(x, c, ln_mix_g, ln_ffn_g, w_mod, b_mod, w_in, cmp_pe_k, cmp_pe_v, cmp_wk1, cmp_wk2,
           cmp_wv1, cmp_wv2, conv_w, norm_attn_g, norm_conv_g, w_out, peer_wq, peer_keys, peer_u,
           peer_v, rel_table, ln_final_g):
    B, S, _ = x.shape
    n = B * S
    assert w_mod.shape[0] == 1 and S % TM_PROJ == 0 and S % TQ == 0 and n % TT == 0
    x2 = x.reshape(n, D_MODEL)
    perm = _attn_perm()

    mod3 = _modulation(c, w_mod[0], b_mod[0]).reshape(B, 6, D_MODEL)

    w = w_in[0]
    q_cols = w[:, :NSA_WIDTH][:, perm]
    kv_cols = w[:, NSA_WIDTH:NSA_WIDTH + 6 * KV_W]
    g0 = NSA_WIDTH + 6 * KV_W
    gate_cols = jnp.pad(w[:, g0:g0 + NSA_HEADS * 3], ((0, 0), (0, LANES - NSA_HEADS * 3)))
    conv_cols = w[:, g0 + NSA_HEADS * 3:]
    w_perm = jnp.concatenate([q_cols, kv_cols, gate_cols, conv_cols], axis=1).astype(jnp.bfloat16)

    q4, kc, vc, ks, vs, kw, vw, gates, conv_n = _in_projection(
        x2, mod3, ln_mix_g[0].reshape(1, D_MODEL), w_perm, conv_w[0].reshape(CONV_K, CONV_WIDTH),
        norm_conv_g[0].reshape(1, CONV_WIDTH), S)

    ncp = S // CMP_STRIDE
    tok = jnp.stack([kc, vc]).reshape(2, B, ncp, CMP_STRIDE, NSA_KV_HEADS, HEAD_DIM)
    tok = tok.transpose(0, 1, 4, 2, 3, 5).reshape(2, B, NSA_KV_HEADS, ncp, CMP_STRIDE * HEAD_DIM)
    cmp_kv = _compress(tok,
                       jnp.stack([cmp_pe_k[0], cmp_pe_v[0]]).reshape(2, 1, CMP_BLK * HEAD_DIM),
                       jnp.stack([cmp_wk1[0], cmp_wv1[0]]), jnp.stack([cmp_wk2[0], cmp_wv2[0]]))

    b_cmp, b_near = _bias_tables(rel_table, S)
    attn4 = _attention(q4, ks, vs, kw, vw, cmp_kv, gates, b_cmp, b_near, _cmp_to_slc(S),
                       norm_attn_g[0][perm].reshape(NSA_REP, 1, LANES), B, S)

    w_out_p = jnp.concatenate([w_out[0][:NSA_WIDTH][perm], w_out[0][NSA_WIDTH:]], axis=0)
    half = PEER_QDIM // 2
    keys = peer_keys[0].reshape(2 * PEER_HEADS, PEER_NKEYS, half)
    lo = jnp.pad(keys, ((0, 0), (0, 0), (0, half)))
    hi = jnp.pad(keys, ((0, 0), (0, 0), (half, 0)))
    is_hi = (jnp.arange(2 * PEER_HEADS) % 2 == 1)[:, None, None]
    keys_x = jnp.where(is_hi, hi, lo).astype(jnp.bfloat16)
    x1, h2, eid_t, gate_t = _out_proj_route(
        attn4, conv_n, x2, mod3, w_out_p.astype(jnp.bfloat16), ln_ffn_g[0].reshape(1, D_MODEL),
        peer_wq[0].astype(jnp.bfloat16), keys_x, S)

    idx_tiles = eid_t.T.reshape(n // TT, TT * PEER_E)
    w_act = _peer_act(idx_tiles, h2.reshape(n, ROW_CHUNKS, LANES), gate_t.T, _pack_table(peer_u[0]))
    peer = _peer_out(idx_tiles, w_act, _pack_table(peer_v[0])).reshape(n, D_MODEL)

    out = _final(x1, peer, mod3, ln_final_g.reshape(1, D_MODEL), S)
    return out.reshape(B, S, D_MODEL)
```

```python
import functools
import math

import numpy as np
import jax
import jax.numpy as jnp
from jax import lax
from jax.experimental import pallas as pl
from jax.experimental.pallas import tpu as pltpu

D_MODEL = 1024
HEAD_DIM = 64
NSA_HEADS = 8
NSA_KV_HEADS = 2
NSA_REP = NSA_HEADS // NSA_KV_HEADS
NSA_WIDTH = NSA_HEADS * HEAD_DIM
KV_W = NSA_KV_HEADS * HEAD_DIM
CONV_WIDTH = D_MODEL - NSA_WIDTH
CONV_K = 3
CMP_BLK = 32
CMP_STRIDE = 16
SEL_BLK = 64
SEL_TOPN = 8
WINDOW = 512
FORCE_SCORE = 1e4
REL_BUCKETS = 32
REL_MAX_DIST = 128
PEER_HEADS = 8
PEER_NKEYS = 128
PEER_EXPERTS = PEER_NKEYS * PEER_NKEYS
PEER_QDIM = 128
PEER_TOPK = 16
PEER_E = PEER_HEADS * PEER_TOPK
EPS = 1e-6

LANES = 128
SUBLANES = 8
ROW_CHUNKS = D_MODEL // LANES
VMEM_LIMIT = 56 * 1024 * 1024

TM_PROJ = 512
TM_OUT = 256
TQ = 256
TK = 256
TT = 64
NEG = -1e30
HIGHEST = lax.Precision.HIGHEST

C_Q = 0
C_KV = C_Q + NSA_WIDTH
C_GATE = C_KV + 6 * KV_W
C_CONV = C_GATE + LANES
N_COLS = C_CONV + 3 * CONV_WIDTH


def _params(sem):
    return pltpu.CompilerParams(dimension_semantics=sem, vmem_limit_bytes=VMEM_LIMIT)


def _dot(a, b, precision=None):
    return jnp.dot(a, b, preferred_element_type=jnp.float32, precision=precision)


def _dot_nt(a, b, precision=None):
    return lax.dot_general(a, b, (((1,), (1,)), ((), ())),
                           preferred_element_type=jnp.float32, precision=precision)


def _rms(x, n):
    return x * lax.rsqrt(jnp.sum(x * x, axis=-1, keepdims=True) * (1.0 / n) + EPS)


def _mod_kernel(c_ref, w_ref, b_ref, o_ref):
    c = c_ref[...]
    act = c * jax.nn.sigmoid(c)
    o_ref[...] = _dot(act, w_ref[...], HIGHEST) + b_ref[...]


def _modulation(c, w_mod, b_mod):
    B = c.shape[0]
    n = w_mod.shape[1]
    bn = D_MODEL
    return pl.pallas_call(
        _mod_kernel,
        grid=(n // bn,),
        in_specs=[pl.BlockSpec((B, D_MODEL), lambda j: (0, 0)),
                  pl.BlockSpec((D_MODEL, bn), lambda j: (0, j)),
                  pl.BlockSpec((1, bn), lambda j: (0, j))],
        out_specs=pl.BlockSpec((B, bn), lambda j: (0, j)),
        out_shape=jax.ShapeDtypeStruct((B, n), jnp.float32),
        compiler_params=_params(("arbitrary",)),
        name="modulation",
    )(c, w_mod, b_mod.reshape(1, n))


def _inproj_kernel(tiles_per_seq, x_ref, mod_ref, g_ref, w_ref, cw_ref, cg_ref,
                   q_ref, kc_ref, vc_ref, ks_ref, vs_ref, kw_ref, vw_ref, gate_ref, conv_ref,
                   carry_ref):
    i = pl.program_id(0)
    x = x_ref[...]
    sh = mod_ref[0, 0:1, :]
    sc = mod_ref[0, 1:2, :]
    h = (_rms(x, D_MODEL) * g_ref[...]) * (1.0 + sc) + sh
    proj = _dot(h.astype(jnp.bfloat16), w_ref[...])

    for r in range(NSA_REP):
        q_ref[r] = (proj[:, C_Q + r * LANES:C_Q + (r + 1) * LANES] * (HEAD_DIM ** -0.5)
                    ).astype(jnp.bfloat16)
    kv = lambda k: proj[:, C_KV + k * KV_W:C_KV + (k + 1) * KV_W]
    kc_ref[...] = kv(0)
    vc_ref[...] = kv(1)
    ks_ref[...] = kv(2).astype(jnp.bfloat16)
    vs_ref[...] = kv(3).astype(jnp.bfloat16)
    kw_ref[...] = kv(4).astype(jnp.bfloat16)
    vw_ref[...] = kv(5).astype(jnp.bfloat16)
    gate_ref[...] = jax.nn.sigmoid(proj[:, C_GATE:C_GATE + LANES])

    cb = proj[:, C_CONV:C_CONV + CONV_WIDTH]
    cc = proj[:, C_CONV + CONV_WIDTH:C_CONV + 2 * CONV_WIDTH]
    ch = proj[:, C_CONV + 2 * CONV_WIDTH:C_CONV + 3 * CONV_WIDTH]
    z = cc * ch
    tm = z.shape[0]

    @pl.when(i % tiles_per_seq == 0)
    def _():
        carry_ref[...] = jnp.zeros_like(carry_ref)

    prev1 = carry_ref[SUBLANES - 1:SUBLANES, :]
    prev2 = carry_ref[SUBLANES - 2:SUBLANES - 1, :]
    row = lax.broadcasted_iota(jnp.int32, (tm, CONV_WIDTH), 0)
    z1 = jnp.where(row == 0, prev1, pltpu.roll(z, 1, 0))
    z2 = pltpu.roll(z, 2, 0)
    z2 = jnp.where(row == 0, prev2, jnp.where(row == 1, prev1, z2))
    carry_ref[...] = z[tm - SUBLANES:, :]
    y = cw_ref[0:1, :] * z2 + cw_ref[1:2, :] * z1 + cw_ref[2:3, :] * z
    conv_ref[...] = (_rms(cb * y, CONV_WIDTH) * cg_ref[...]).astype(jnp.bfloat16)


def _in_projection(x2, mod3, ln_g, w_perm, conv_w, conv_g, seq):
    n = x2.shape[0]
    tm = TM_PROJ
    tps = seq // tm
    row = lambda i: (i, 0)
    kv_f32 = jax.ShapeDtypeStruct((n, KV_W), jnp.float32)
    kv_bf = jax.ShapeDtypeStruct((n, KV_W), jnp.bfloat16)
    return pl.pallas_call(
        functools.partial(_inproj_kernel, tps),
        grid=(n // tm,),
        in_specs=[pl.BlockSpec((tm, D_MODEL), row),
                  pl.BlockSpec((1, 6, D_MODEL), lambda i: (i // tps, 0, 0)),
                  pl.BlockSpec((1, D_MODEL), lambda i: (0, 0)),
                  pl.BlockSpec((D_MODEL, N_COLS), lambda i: (0, 0)),
                  pl.BlockSpec((CONV_K, CONV_WIDTH), lambda i: (0, 0)),
                  pl.BlockSpec((1, CONV_WIDTH), lambda i: (0, 0))],
        out_specs=[pl.BlockSpec((NSA_REP, tm, LANES), lambda i: (0, i, 0))]
                  + [pl.BlockSpec((tm, KV_W), row)] * 6
                  + [pl.BlockSpec((tm, LANES), row), pl.BlockSpec((tm, CONV_WIDTH), row)],
        out_shape=[jax.ShapeDtypeStruct((NSA_REP, n, LANES), jnp.bfloat16),
                   kv_f32, kv_f32, kv_bf, kv_bf, kv_bf, kv_bf,
                   jax.ShapeDtypeStruct((n, LANES), jnp.float32),
                   jax.ShapeDtypeStruct((n, CONV_WIDTH), jnp.bfloat16)],
        scratch_shapes=[pltpu.VMEM((SUBLANES, CONV_WIDTH), jnp.float32)],
        compiler_params=_params(("arbitrary",)),
        name="in_projection",
    )(x2, mod3, ln_g, w_perm, conv_w, conv_g)


def _compress_kernel(tok_ref, pe_ref, w1_ref, w2_ref, o_ref):
    half = (CMP_BLK // 2) * HEAD_DIM
    w1a = w1_ref[0, 0:half, :]
    w1b = w1_ref[0, half:2 * half, :]
    pe_term = _dot(pe_ref[0], w1_ref[0], HIGHEST)
    ncp = tok_ref.shape[3]
    row = lax.broadcasted_iota(jnp.int32, (ncp, HEAD_DIM), 0)
    for g in range(NSA_KV_HEADS):
        t = tok_ref[0, 0, g]
        a = _dot(t, w1a, HIGHEST)
        b = _dot(t, w1b, HIGHEST)
        pre = a + pltpu.roll(b, ncp - 1, 0) + pe_term
        out = _dot(jax.nn.gelu(pre), w2_ref[0], HIGHEST)
        o_ref[0, 0, :, g * HEAD_DIM:(g + 1) * HEAD_DIM] = jnp.where(row < ncp - 1, out, 0.0)


def _compress(tok, pe, w1, w2):
    _, B, G, ncp, width = tok.shape
    return pl.pallas_call(
        _compress_kernel,
        grid=(2, B),
        in_specs=[pl.BlockSpec((1, 1, G, ncp, width), lambda k, b: (k, b, 0, 0, 0)),
                  pl.BlockSpec((1, 1, CMP_BLK * HEAD_DIM), lambda k, b: (k, 0, 0)),
                  pl.BlockSpec((1, CMP_BLK * HEAD_DIM, HEAD_DIM), lambda k, b: (k, 0, 0)),
                  pl.BlockSpec((1, HEAD_DIM, HEAD_DIM), lambda k, b: (k, 0, 0))],
        out_specs=pl.BlockSpec((1, 1, ncp, KV_W), lambda k, b: (k, b, 0, 0)),
        out_shape=jax.ShapeDtypeStruct((2, B, ncp, KV_W), jnp.float32),
        compiler_params=_params(("arbitrary", "arbitrary")),
        name="compress_kv",
    )(tok, pe, w1, w2)


def _softmax_step(s, msk, v, m_old, l_old, acc_old):
    s = jnp.where(msk, s, NEG)
    m_new = jnp.maximum(m_old, jnp.max(s, axis=-1, keepdims=True))
    p = jnp.where(msk, jnp.exp(s - m_new), 0.0)
    alpha = jnp.exp(m_old - m_new)
    l_new = alpha * l_old + jnp.sum(p, axis=-1, keepdims=True)
    acc_new = alpha * acc_old + _dot(p.astype(jnp.bfloat16), v)
    return m_new, l_new, acc_new


def _attn_kernel(n_slc, q_ref, ks_ref, vs_ref, kw_ref, vw_ref, cmp_ref, gate_ref,
                 bcmp_ref, bnear_ref, c2s_ref, ng_ref, o_ref, og0_ref):
    qi = pl.program_id(1)
    t0 = qi * TQ
    rows = NSA_REP * TQ
    ncp = cmp_ref.shape[2]
    q4 = q_ref[...].reshape(rows, LANES)
    lane = lax.broadcasted_iota(jnp.int32, (1, LANES), 1)
    tq_col = t0 + lax.broadcasted_iota(jnp.int32, (TQ, 1), 0)
    t_col = jnp.concatenate([tq_col] * NSA_REP, axis=0)
    rowt = t_col - t0
    colk = lax.broadcasted_iota(jnp.int32, (1, TK), 1)
    gates = gate_ref[...]
    n_near = bnear_ref.shape[1] - 1

    for g in range(NSA_KV_HEADS):
        in_g = (lane // HEAD_DIM) == g
        q = jnp.where(in_g, q4, jnp.zeros_like(q4))

        kc = cmp_ref[0, 0]
        vc = cmp_ref[1, 0]
        s = _dot_nt(q.astype(jnp.float32), kc, HIGHEST) + bcmp_ref[g, 0]
        n_idx = lax.broadcasted_iota(jnp.int32, (1, ncp), 1)
        valid = (n_idx * CMP_STRIDE + (CMP_BLK - 1) <= t_col) & (n_idx < ncp - 1)
        s = jnp.where(valid, s, NEG)
        e = jnp.where(valid, jnp.exp(s - jnp.max(s, axis=-1, keepdims=True)), 0.0)
        den = jnp.sum(e, axis=-1, keepdims=True)
        p_cmp = e / jnp.maximum(den, 1e-30)
        o_cmp = _dot(p_cmp, vc, HIGHEST)

        p_sum = p_cmp[0:TQ]
        for r in range(1, NSA_REP):
            p_sum = p_sum + p_cmp[r * TQ:(r + 1) * TQ]
        imp = _dot(p_sum, c2s_ref[...], HIGHEST)
        lane_f = lane.astype(jnp.float32)
        cur = tq_col // SEL_BLK
        forced = (lane == 0) | (lane == cur) | (lane == cur - 1)
        allowed = lane * SEL_BLK <= tq_col
        score = jnp.where(forced, FORCE_SCORE, jnp.where(allowed, imp, -jnp.inf))
        real = lane < n_slc
        taken = jnp.logical_not(real) | jnp.zeros((TQ, LANES), jnp.bool_)
        for _ in range(min(SEL_TOPN, n_slc)):
            eff = jnp.where(taken, -jnp.inf, score)
            mx = jnp.max(eff, axis=-1, keepdims=True)
            cand = jnp.where((eff == mx) & jnp.logical_not(taken), lane_f, float(LANES))
            first = jnp.min(cand, axis=-1, keepdims=True)
            taken = taken | (lane_f == first)
        sel = jnp.where(taken & real, 1.0, 0.0).astype(jnp.bfloat16)
        blk_row = lax.broadcasted_iota(jnp.int32, (LANES, TK), 0)
        blk_col = lax.broadcasted_iota(jnp.int32, (LANES, TK), 1) // SEL_BLK

        def sel_mask(kt):
            expand = jnp.where(blk_row == blk_col + kt * (TK // SEL_BLK), 1.0, 0.0)
            one = _dot(sel, expand.astype(jnp.bfloat16))
            return jnp.concatenate([one] * NSA_REP, axis=0) > 0.5

        def scores(k_ref, kt):
            k = k_ref[pl.ds(pl.multiple_of(kt * TK, TK), TK), :]
            d = qi - kt
            bias = bnear_ref[g, jnp.minimum(d, n_near)]
            dist = d * TQ + rowt - colk
            return _dot_nt(q, k) + bias, dist

        init = (jnp.full((rows, 1), NEG, jnp.float32), jnp.zeros((rows, 1), jnp.float32),
                jnp.zeros((rows, LANES), jnp.float32))

        def slc_step(kt, carry):
            s, dist = scores(ks_ref, kt)
            v = vs_ref[pl.ds(pl.multiple_of(kt * TK, TK), TK), :]
            return _softmax_step(s, sel_mask(kt) & (dist >= 0), v, *carry)

        def win_step(kt, carry):
            s, dist = scores(kw_ref, kt)
            v = vw_ref[pl.ds(pl.multiple_of(kt * TK, TK), TK), :]
            return _softmax_step(s, (dist >= 0) & (dist < WINDOW), v, *carry)

        _, l_s, acc_s = lax.fori_loop(0, qi + 1, slc_step, init)
        _, l_w, acc_w = lax.fori_loop(jnp.maximum(qi - WINDOW // TK, 0), qi + 1, win_step, init)
        o_slc = acc_s / jnp.maximum(l_s, 1e-30)
        o_win = acc_w / jnp.maximum(l_w, 1e-30)

        for r in range(NSA_REP):
            c0 = (g * NSA_REP + r) * 3
            sl = slice(r * TQ, (r + 1) * TQ)
            o = (gates[:, c0:c0 + 1] * o_cmp[sl] + gates[:, c0 + 1:c0 + 2] * o_slc[sl]
                 + gates[:, c0 + 2:c0 + 3] * o_win[sl])
            if g == 0:
                og0_ref[r] = o
            else:
                og0_ref[r] = jnp.where(in_g, o, og0_ref[r])

    ss = jnp.zeros((TQ, 1), jnp.float32)
    for r in range(NSA_REP):
        o = og0_ref[r]
        ss = ss + jnp.sum(o * o, axis=-1, keepdims=True)
    inv = lax.rsqrt(ss * (1.0 / NSA_WIDTH) + EPS)
    for r in range(NSA_REP):
        o_ref[r] = (og0_ref[r] * inv * ng_ref[r]).astype(jnp.bfloat16)


def _attention(q4, ks, vs, kw, vw, cmp_kv, gates, bias_cmp, bias_near, c2s, norm_g, batch, seq):
    n = batch * seq
    nq = seq // TQ
    ncp = cmp_kv.shape[2]
    n_slc = seq // SEL_BLK
    kv_spec = pl.BlockSpec((seq, KV_W), lambda b, i: (b, 0))
    return pl.pallas_call(
        functools.partial(_attn_kernel, n_slc),
        grid=(batch, nq),
        in_specs=[pl.BlockSpec((NSA_REP, TQ, LANES), lambda b, i: (0, b * nq + i, 0)),
                  kv_spec, kv_spec, kv_spec, kv_spec,
                  pl.BlockSpec((2, 1, ncp, KV_W), lambda b, i: (0, b, 0, 0)),
                  pl.BlockSpec((TQ, LANES), lambda b, i: (b * nq + i, 0)),
                  pl.BlockSpec((NSA_KV_HEADS, 1, NSA_REP * TQ, ncp), lambda b, i: (0, i, 0, 0)),
                  pl.BlockSpec(bias_near.shape, lambda b, i: (0, 0, 0, 0)),
                  pl.BlockSpec(c2s.shape, lambda b, i: (0, 0)),
                  pl.BlockSpec((NSA_REP, 1, LANES), lambda b, i: (0, 0, 0))],
        out_specs=pl.BlockSpec((NSA_REP, TQ, LANES), lambda b, i: (0, b * nq + i, 0)),
        out_shape=jax.ShapeDtypeStruct((NSA_REP, n, LANES), jnp.bfloat16),
        scratch_shapes=[pltpu.VMEM((NSA_REP, TQ, LANES), jnp.float32)],
        compiler_params=_params(("arbitrary", "arbitrary")),
        name="sparse_attention",
    )(q4, ks, vs, kw, vw, cmp_kv, gates, bias_cmp, bias_near, c2s, norm_g)


def _topk_rows(s, k):
    nrow = s.shape[0]
    rowf = lax.broadcasted_iota(jnp.int32, s.shape, 0).astype(jnp.float32)
    slot = lax.broadcasted_iota(jnp.int32, (k, s.shape[1]), 0)
    vals = jnp.zeros((k, s.shape[1]), jnp.float32)
    idxs = jnp.zeros((k, s.shape[1]), jnp.float32)
    for j in range(k):
        m = jnp.max(s, axis=0, keepdims=True)
        first = jnp.min(jnp.where(s == m, rowf, float(nrow)), axis=0, keepdims=True)
        vals = jnp.where(slot == j, m, vals)
        idxs = jnp.where(slot == j, first, idxs)
        s = jnp.where(rowf == first, -jnp.inf, s)
    return vals, idxs


def _route_kernel(attn_ref, conv_ref, x_ref, mod_ref, wout_ref, g_ref, wq_ref, keys_ref,
                  x1_ref, h2_ref, eid_ref, gate_ref):
    mixed = jnp.concatenate([attn_ref[r] for r in range(NSA_REP)] + [conv_ref[...]], axis=-1)
    x1 = x_ref[...] + mod_ref[0, 2:3, :] * _dot(mixed, wout_ref[...])
    x1_ref[...] = x1
    h2 = (_rms(x1, D_MODEL) * g_ref[...]) * (1.0 + mod_ref[0, 4:5, :]) + mod_ref[0, 3:4, :]
    h2_ref[...] = h2
    qp = _dot(h2.astype(jnp.bfloat16), wq_ref[...]).astype(jnp.bfloat16)

    tm = qp.shape[0]
    slot = lax.broadcasted_iota(jnp.int32, (PEER_TOPK, tm), 0)
    K = PEER_TOPK
    n_mid = K // 2 - 1
    row = lax.broadcasted_iota(jnp.int32, (K + SUBLANES * n_mid + SUBLANES, tm), 0)
    mid_a = (row - K) // SUBLANES + 1
    tail = row >= K + SUBLANES * n_mid
    cand_a = jnp.where(row < K, 0, jnp.where(tail, K // 2 + row % SUBLANES, mid_a))
    cand_b = jnp.where(row < K, row, jnp.where(tail, 0, row % SUBLANES))
    in_stair = (cand_a + 1) * (cand_b + 1) <= K
    rowf = (cand_a * K + cand_b).astype(jnp.float32)

    def staircase(first_half, second_half):
        parts = [first_half[0:1] + second_half]
        parts += [first_half[a:a + 1] + second_half[0:SUBLANES] for a in range(1, n_mid + 1)]
        parts += [first_half[K // 2:K] + second_half[0:1]]
        return jnp.concatenate(parts, axis=0)

    for h in range(PEER_HEADS):
        blk = qp[:, h * PEER_QDIM:(h + 1) * PEER_QDIM]
        v1, i1 = _topk_rows(_dot_nt(keys_ref[2 * h], blk), PEER_TOPK)
        v2, i2 = _topk_rows(_dot_nt(keys_ref[2 * h + 1], blk), PEER_TOPK)
        cand = jnp.where(in_stair, staircase(v1, v2), -jnp.inf)
        eid = staircase(i1 * float(PEER_NKEYS), i2)
        top_s = jnp.zeros((PEER_TOPK, tm), jnp.float32)
        top_e = jnp.zeros((PEER_TOPK, tm), jnp.float32)
        for j in range(PEER_TOPK):
            m = jnp.max(cand, axis=0, keepdims=True)
            first = jnp.min(jnp.where(cand == m, rowf, 1e9), axis=0, keepdims=True)
            hit = rowf == first
            e = jnp.sum(jnp.where(hit, eid, 0.0), axis=0, keepdims=True)
            top_s = jnp.where(slot == j, m, top_s)
            top_e = jnp.where(slot == j, e, top_e)
            cand = jnp.where(hit, -jnp.inf, cand)
        ex = jnp.exp(top_s - jnp.max(top_s, axis=0, keepdims=True))
        gate_ref[h * PEER_TOPK:(h + 1) * PEER_TOPK, :] = ex / jnp.sum(ex, axis=0, keepdims=True)
        eid_ref[h * PEER_TOPK:(h + 1) * PEER_TOPK, :] = top_e.astype(jnp.int32)


def _out_proj_route(attn4, conv_n, x2, mod3, w_out_p, ln_g, wq, keys_x, seq):
    n = x2.shape[0]
    tm = TM_OUT
    tps = seq // tm
    row = lambda i: (i, 0)
    const2 = lambda i: (0, 0)
    return pl.pallas_call(
        _route_kernel,
        grid=(n // tm,),
        in_specs=[pl.BlockSpec((NSA_REP, tm, LANES), lambda i: (0, i, 0)),
                  pl.BlockSpec((tm, CONV_WIDTH), row),
                  pl.BlockSpec((tm, D_MODEL), row),
                  pl.BlockSpec((1, 6, D_MODEL), lambda i: (i // tps, 0, 0)),
                  pl.BlockSpec((D_MODEL, D_MODEL), const2),
                  pl.BlockSpec((1, D_MODEL), const2),
                  pl.BlockSpec((D_MODEL, PEER_HEADS * PEER_QDIM), const2),
                  pl.BlockSpec((2 * PEER_HEADS, PEER_NKEYS, PEER_QDIM), lambda i: (0, 0, 0))],
        out_specs=[pl.BlockSpec((tm, D_MODEL), row), pl.BlockSpec((tm, D_MODEL), row),
                   pl.BlockSpec((PEER_E, tm), lambda i: (0, i)),
                   pl.BlockSpec((PEER_E, tm), lambda i: (0, i))],
        out_shape=[jax.ShapeDtypeStruct((n, D_MODEL), jnp.float32),
                   jax.ShapeDtypeStruct((n, D_MODEL), jnp.float32),
                   jax.ShapeDtypeStruct((PEER_E, n), jnp.int32),
                   jax.ShapeDtypeStruct((PEER_E, n), jnp.float32)],
        compiler_params=_params(("arbitrary",)),
        name="out_proj_route",
    )(attn4, conv_n, x2, mod3, w_out_p, ln_g, wq, keys_x)


ROWS_PER_EXPERT = ROW_CHUNKS // 2


IDX_TILE = TT * PEER_E


def _index_copy(idx_hbm, idx_smem, sem, tile, slot):
    dst = idx_smem.at[pl.ds(pl.multiple_of(slot * IDX_TILE, IDX_TILE), IDX_TILE)]
    return pltpu.make_async_copy(idx_hbm.at[tile], dst, sem.at[slot])


def _fetch_indices(idx_hbm, idx_smem, sem):
    i = pl.program_id(0)
    slot = i % 2

    @pl.when(i == 0)
    def _():
        _index_copy(idx_hbm, idx_smem, sem, 0, 0).start()

    _index_copy(idx_hbm, idx_smem, sem, i, slot).wait()

    @pl.when(i + 1 < pl.num_programs(0))
    def _():
        _index_copy(idx_hbm, idx_smem, sem, i + 1, 1 - slot).start()

    return slot


def _gather_rows(slot, t, idx_smem, tab_ref, rows_ref):
    tok_idx = idx_smem.at[pl.ds(pl.multiple_of(slot * IDX_TILE + t * PEER_E, PEER_E), PEER_E)]
    for k in range(PEER_E):
        e = tok_idx[k]
        rows_ref[k * ROWS_PER_EXPERT:(k + 1) * ROWS_PER_EXPERT, :] = tab_ref[e]


def _token_pairs(gather, compute, rows_a, rows_b):
    gather(0, rows_a)

    def pair(i, _):
        t = 2 * i
        gather(t + 1, rows_b)
        compute(t, rows_a)
        gather(jnp.minimum(t + 2, TT - 1), rows_a)
        compute(t + 1, rows_b)
        return 0

    lax.fori_loop(0, TT // 2, pair, 0)


def _split_bf16(x):
    hi = x.astype(jnp.bfloat16).astype(jnp.float32)
    return jnp.concatenate([hi, x - hi], axis=0).astype(jnp.bfloat16)


def _expert_of_row():
    n = lax.broadcasted_iota(jnp.int32, (PEER_E * ROW_CHUNKS, PEER_E), 0) // ROW_CHUNKS
    k = lax.broadcasted_iota(jnp.int32, (PEER_E * ROW_CHUNKS, PEER_E), 1)
    return jnp.where(n == k, 1.0, 0.0)


def _peer_act_kernel(idx_hbm, x_ref, gate_ref, tab_ref, w_ref, idx_smem, sem, rows_a, rows_b, z_ref):
    slot = _fetch_indices(idx_hbm, idx_smem, sem)
    m_i = lax.broadcasted_iota(jnp.int32, (2 * ROW_CHUNKS, PEER_E * ROW_CHUNKS), 0) % ROW_CHUNKS
    n_i = lax.broadcasted_iota(jnp.int32, (2 * ROW_CHUNKS, PEER_E * ROW_CHUNKS), 1) % ROW_CHUNKS
    chunk_match = m_i == n_i

    def compute(t, rows_ref):
        x16 = _split_bf16(x_ref[t])
        u = pltpu.bitcast(rows_ref[...], jnp.bfloat16)
        r = _dot_nt(x16, u)
        z_ref[pl.ds(t, 1), :] = jnp.sum(jnp.where(chunk_match, r, 0.0), axis=0, keepdims=True)

    gather = lambda t, rows_ref: _gather_rows(slot, t, idx_smem, tab_ref, rows_ref)
    _token_pairs(gather, compute, rows_a, rows_b)
    act = _dot(z_ref[...], _expert_of_row(), HIGHEST)
    w_ref[...] = gate_ref[...] * jax.nn.gelu(act)


def _peer_out_kernel(idx_hbm, w_ref, tab_ref, o_ref, idx_smem, sem, rows_a, rows_b, w8_ref):
    slot = _fetch_indices(idx_hbm, idx_smem, sem)
    w8_ref[...] = _dot_nt(w_ref[...], _expert_of_row(), HIGHEST)
    m_i = lax.broadcasted_iota(jnp.int32, (ROW_CHUNKS, PEER_E * ROW_CHUNKS), 0)
    n_i = lax.broadcasted_iota(jnp.int32, (ROW_CHUNKS, PEER_E * ROW_CHUNKS), 1) % ROW_CHUNKS
    chunk_match = m_i == n_i

    def compute(t, rows_ref):
        lhs = _split_bf16(jnp.where(chunk_match, w8_ref[pl.ds(t, 1), :], 0.0))
        v = pltpu.bitcast(rows_ref[...], jnp.bfloat16)
        o = _dot(lhs, v)
        o_ref[t] = o[0:ROW_CHUNKS] + o[ROW_CHUNKS:2 * ROW_CHUNKS]

    gather = lambda t, rows_ref: _gather_rows(slot, t, idx_smem, tab_ref, rows_ref)
    _token_pairs(gather, compute, rows_a, rows_b)


def _peer_scratch():
    rows = pltpu.VMEM((PEER_E * ROWS_PER_EXPERT, LANES), jnp.int32)
    return [pltpu.SMEM((2 * IDX_TILE,), jnp.int32), pltpu.SemaphoreType.DMA((2,)), rows, rows,
            pltpu.VMEM((TT, PEER_E * ROW_CHUNKS), jnp.float32)]


def _peer_act(idx_tiles, x8, gates, tab_u):
    n = x8.shape[0]
    return pl.pallas_call(
        _peer_act_kernel,
        grid=(n // TT,),
        in_specs=[pl.BlockSpec(memory_space=pl.ANY),
                  pl.BlockSpec((TT, ROW_CHUNKS, LANES), lambda i: (i, 0, 0)),
                  pl.BlockSpec((TT, PEER_E), lambda i: (i, 0)),
                  pl.BlockSpec(tab_u.shape, lambda i: (0, 0, 0))],
        out_specs=pl.BlockSpec((TT, PEER_E), lambda i: (i, 0)),
        out_shape=jax.ShapeDtypeStruct((n, PEER_E), jnp.float32),
        scratch_shapes=_peer_scratch(),
        compiler_params=_params(("arbitrary",)),
        name="peer_activations",
    )(idx_tiles, x8, gates, tab_u)


def _peer_out(idx_tiles, w, tab_v):
    n = w.shape[0]
    return pl.pallas_call(
        _peer_out_kernel,
        grid=(n // TT,),
        in_specs=[pl.BlockSpec(memory_space=pl.ANY),
                  pl.BlockSpec((TT, PEER_E), lambda i: (i, 0)),
                  pl.BlockSpec(tab_v.shape, lambda i: (0, 0, 0))],
        out_specs=pl.BlockSpec((TT, ROW_CHUNKS, LANES), lambda i: (i, 0, 0)),
        out_shape=jax.ShapeDtypeStruct((n, ROW_CHUNKS, LANES), jnp.float32),
        scratch_shapes=_peer_scratch(),
        compiler_params=_params(("arbitrary",)),
        name="peer_combine",
    )(idx_tiles, w, tab_v)


def _pack_table(tab):
    bits = lax.bitcast_convert_type(tab.astype(jnp.bfloat16), jnp.uint16).astype(jnp.uint32)
    bits = bits.reshape(tab.shape[0], ROWS_PER_EXPERT, 2, LANES)
    words = bits[:, :, 0, :] | (bits[:, :, 1, :] << 16)
    return lax.bitcast_convert_type(words, jnp.int32)


def _final_kernel(x1_ref, p_ref, mod_ref, g_ref, o_ref):
    x2 = x1_ref[...] + mod_ref[0, 5:6, :] * p_ref[...]
    o_ref[...] = _rms(x2, D_MODEL) * g_ref[...]


def _final(x1, peer, mod3, g, seq):
    n = x1.shape[0]
    tm = TM_PROJ
    tps = seq // tm
    row = lambda i: (i, 0)
    return pl.pallas_call(
        _final_kernel,
        grid=(n // tm,),
        in_specs=[pl.BlockSpec((tm, D_MODEL), row), pl.BlockSpec((tm, D_MODEL), row),
                  pl.BlockSpec((1, 6, D_MODEL), lambda i: (i // tps, 0, 0)),
                  pl.BlockSpec((1, D_MODEL), lambda i: (0, 0))],
        out_specs=pl.BlockSpec((tm, D_MODEL), row),
        out_shape=jax.ShapeDtypeStruct((n, D_MODEL), jnp.float32),
        compiler_params=_params(("arbitrary",)),
        name="final_norm",
    )(x1, peer, mod3, g)


def _attn_perm():
    idx = []
    for r in range(NSA_REP):
        for g in range(NSA_KV_HEADS):
            h = g * NSA_REP + r
            idx.extend(range(h * HEAD_DIM, (h + 1) * HEAD_DIM))
    return np.asarray(idx, np.int32)


def _rel_bucket_np(dist):
    n = np.maximum(dist, 0)
    exact = REL_BUCKETS // 2
    log_ratio = (np.log(np.maximum(n, 1).astype(np.float32) / np.float32(exact))
                 / np.float32(math.log(REL_MAX_DIST / exact)))
    large = exact + (log_ratio * np.float32(REL_BUCKETS - exact)).astype(np.int32)
    return np.where(n < exact, n, np.minimum(large, REL_BUCKETS - 1)).astype(np.int32)


def _bias_tables(rel_table, seq):
    nq = seq // TQ
    ncp = seq // CMP_STRIDE

    def lookup(bucket):
        onehot = (jnp.asarray(bucket)[..., None] == jnp.arange(REL_BUCKETS)).astype(jnp.float32)
        out = jnp.einsum("...b,bh->...h", onehot, rel_table, precision=HIGHEST)
        return out.reshape(bucket.shape + (NSA_KV_HEADS, NSA_REP))
    n_near = -(-REL_MAX_DIST // TQ) + 1
    r_i = np.arange(TQ)[:, None]
    c_i = np.arange(TK)[None, :]
    near = np.stack([_rel_bucket_np(d * TQ + r_i - c_i) for d in range(n_near)]
                    + [np.full((TQ, TK), REL_BUCKETS - 1, np.int32)])
    b_near = jnp.transpose(lookup(near), (3, 0, 4, 1, 2))
    b_near = b_near.reshape(NSA_KV_HEADS, n_near + 1, NSA_REP * TQ, TK)
    t_i = np.arange(seq)[:, None]
    n_i = np.arange(ncp)[None, :]
    cmp_b = _rel_bucket_np(t_i - (n_i * CMP_STRIDE + CMP_BLK - 1))
    b_cmp = jnp.transpose(lookup(cmp_b), (2, 3, 0, 1))
    b_cmp = b_cmp.reshape(NSA_KV_HEADS, NSA_REP, nq, TQ, ncp).transpose(0, 2, 1, 3, 4)
    b_cmp = b_cmp.reshape(NSA_KV_HEADS, nq, NSA_REP * TQ, ncp)
    return b_cmp, b_near


def _cmp_to_slc(seq):
    ncp = seq // CMP_STRIDE
    n_cmp = (seq - CMP_BLK) // CMP_STRIDE + 1
    n_slc = seq // SEL_BLK
    cs = np.arange(ncp) * CMP_STRIDE
    ss = np.arange(LANES) * SEL_BLK
    ov = (cs[:, None] < ss[None, :] + SEL_BLK) & (cs[:, None] + CMP_BLK > ss[None, :])
    ov &= (np.arange(ncp)[:, None] < n_cmp) & (np.arange(LANES)[None, :] < n_slc)
    return jnp.asarray(ov, jnp.float32)


def kernel(x, c, ln_mix_g, ln_ffn_g, w_mod, b_mod, w_in, cmp_pe_k, cmp_pe_v, cmp_wk1, cmp_wk2,
           cmp_wv1, cmp_wv2, conv_w, norm_attn_g, norm_conv_g, w_out, peer_wq, peer_keys, peer_u,
           peer_v, rel_table, ln_final_g):
    B, S, _ = x.shape
    n = B * S
    assert w_mod.shape[0] == 1 and S % TM_PROJ == 0 and S % TQ == 0 and n % TT == 0
    x2 = x.reshape(n, D_MODEL)
    perm = _attn_perm()

    mod3 = _modulation(c, w_mod[0], b_mod[0]).reshape(B, 6, D_MODEL)

    w = w_in[0]
    q_cols = w[:, :NSA_WIDTH][:, perm]
    kv_cols = w[:, NSA_WIDTH:NSA_WIDTH + 6 * KV_W]
    g0 = NSA_WIDTH + 6 * KV_W
    gate_cols = jnp.pad(w[:, g0:g0 + NSA_HEADS * 3], ((0, 0), (0, LANES - NSA_HEADS * 3)))
    conv_cols = w[:, g0 + NSA_HEADS * 3:]
    w_perm = jnp.concatenate([q_cols, kv_cols, gate_cols, conv_cols], axis=1).astype(jnp.bfloat16)

    q4, kc, vc, ks, vs, kw, vw, gates, conv_n = _in_projection(
        x2, mod3, ln_mix_g[0].reshape(1, D_MODEL), w_perm, conv_w[0].reshape(CONV_K, CONV_WIDTH),
        norm_conv_g[0].reshape(1, CONV_WIDTH), S)

    ncp = S // CMP_STRIDE
    tok = jnp.stack([kc, vc]).reshape(2, B, ncp, CMP_STRIDE, NSA_KV_HEADS, HEAD_DIM)
    tok = tok.transpose(0, 1, 4, 2, 3, 5).reshape(2, B, NSA_KV_HEADS, ncp, CMP_STRIDE * HEAD_DIM)
    cmp_kv = _compress(tok,
                       jnp.stack([cmp_pe_k[0], cmp_pe_v[0]]).reshape(2, 1, CMP_BLK * HEAD_DIM),
                       jnp.stack([cmp_wk1[0], cmp_wv1[0]]), jnp.stack([cmp_wk2[0], cmp_wv2[0]]))

    b_cmp, b_near = _bias_tables(rel_table, S)
    attn4 = _attention(q4, ks, vs, kw, vw, cmp_kv, gates, b_cmp, b_near, _cmp_to_slc(S),
                       norm_attn_g[0][perm].reshape(NSA_REP, 1, LANES), B, S)

    w_out_p = jnp.concatenate([w_out[0][:NSA_WIDTH][perm], w_out[0][NSA_WIDTH:]], axis=0)
    half = PEER_QDIM // 2
    keys = peer_keys[0].reshape(2 * PEER_HEADS, PEER_NKEYS, half)
    lo = jnp.pad(keys, ((0, 0), (0, 0), (0, half)))
    hi = jnp.pad(keys, ((0, 0), (0, 0), (half, 0)))
    is_hi = (jnp.arange(2 * PEER_HEADS) % 2 == 1)[:, None, None]
    keys_x = jnp.where(is_hi, hi, lo).astype(jnp.bfloat16)
    x1, h2, eid_t, gate_t = _out_proj_route(
        attn4, conv_n, x2, mod3, w_out_p.astype(jnp.bfloat16), ln_ffn_g[0].reshape(1, D_MODEL),
        peer_wq[0].astype(jnp.bfloat16), keys_x, S)

    idx_tiles = eid_t.T.reshape(n // TT, TT * PEER_E)
    w_act = _peer_act(idx_tiles, h2.reshape(n, ROW_CHUNKS, LANES), gate_t.T, _pack_table(peer_u[0]))
    peer = _peer_out(idx_tiles, w_act, _pack_table(peer_v[0])).reshape(n, D_MODEL)

    out = _final(x1, peer, mod3, ln_final_g.reshape(1, D_MODEL), S)
    return out.reshape(B, S, D_MODEL)
```

```python
import functools
import math

import numpy as np
import jax
import jax.numpy as jnp
from jax import lax
from jax.experimental import pallas as pl
from jax.experimental.pallas import tpu as pltpu

D_MODEL = 1024
HEAD_DIM = 64
NSA_HEADS = 8
NSA_KV_HEADS = 2
NSA_REP = NSA_HEADS // NSA_KV_HEADS
NSA_WIDTH = NSA_HEADS * HEAD_DIM
KV_W = NSA_KV_HEADS * HEAD_DIM
CONV_WIDTH = D_MODEL - NSA_WIDTH
CONV_K = 3
CMP_BLK = 32
CMP_STRIDE = 16
SEL_BLK = 64
SEL_TOPN = 8
WINDOW = 512
FORCE_SCORE = 1e4
REL_BUCKETS = 32
REL_MAX_DIST = 128
PEER_HEADS = 8
PEER_NKEYS = 128
PEER_EXPERTS = PEER_NKEYS * PEER_NKEYS
PEER_QDIM = 128
PEER_TOPK = 16
PEER_E = PEER_HEADS * PEER_TOPK
EPS = 1e-6

LANES = 128
SUBLANES = 8
ROW_CHUNKS = D_MODEL // LANES
VMEM_LIMIT = 56 * 1024 * 1024

TM_PROJ = 512
TM_OUT = 256
TQ = 256
TK = 256
TT = 64
NEG = -1e30
HIGHEST = lax.Precision.HIGHEST

C_Q = 0
C_KV = C_Q + NSA_WIDTH
C_GATE = C_KV + 6 * KV_W
C_CONV = C_GATE + LANES
N_COLS = C_CONV + 3 * CONV_WIDTH


def _params(sem):
    return pltpu.CompilerParams(dimension_semantics=sem, vmem_limit_bytes=VMEM_LIMIT)


def _dot(a, b, precision=None):
    return jnp.dot(a, b, preferred_element_type=jnp.float32, precision=precision)


def _dot_nt(a, b, precision=None):
    return lax.dot_general(a, b, (((1,), (1,)), ((), ())),
                           preferred_element_type=jnp.float32, precision=precision)


def _rms(x, n):
    return x * lax.rsqrt(jnp.sum(x * x, axis=-1, keepdims=True) * (1.0 / n) + EPS)


def _mod_kernel(c_ref, w_ref, b_ref, o_ref):
    c = c_ref[...]
    act = c * jax.nn.sigmoid(c)
    o_ref[...] = _dot(act, w_ref[...], HIGHEST) + b_ref[...]


def _modulation(c, w_mod, b_mod):
    B = c.shape[0]
    n = w_mod.shape[1]
    bn = D_MODEL
    return pl.pallas_call(
        _mod_kernel,
        grid=(n // bn,),
        in_specs=[pl.BlockSpec((B, D_MODEL), lambda j: (0, 0)),
                  pl.BlockSpec((D_MODEL, bn), lambda j: (0, j)),
                  pl.BlockSpec((1, bn), lambda j: (0, j))],
        out_specs=pl.BlockSpec((B, bn), lambda j: (0, j)),
        out_shape=jax.ShapeDtypeStruct((B, n), jnp.float32),
        compiler_params=_params(("arbitrary",)),
        name="modulation",
    )(c, w_mod, b_mod.reshape(1, n))


def _inproj_kernel(tiles_per_seq, x_ref, mod_ref, g_ref, w_ref, cw_ref, cg_ref,
                   q_ref, kc_ref, vc_ref, ks_ref, vs_ref, kw_ref, vw_ref, gate_ref, conv_ref,
                   carry_ref):
    i = pl.program_id(0)
    x = x_ref[...]
    sh = mod_ref[0, 0:1, :]
    sc = mod_ref[0, 1:2, :]
    h = (_rms(x, D_MODEL) * g_ref[...]) * (1.0 + sc) + sh
    proj = _dot(h.astype(jnp.bfloat16), w_ref[...])

    for r in range(NSA_REP):
        q_ref[r] = (proj[:, C_Q + r * LANES:C_Q + (r + 1) * LANES] * (HEAD_DIM ** -0.5)
                    ).astype(jnp.bfloat16)
    kv = lambda k: proj[:, C_KV + k * KV_W:C_KV + (k + 1) * KV_W]
    kc_ref[...] = kv(0)
    vc_ref[...] = kv(1)
    ks_ref[...] = kv(2).astype(jnp.bfloat16)
    vs_ref[...] = kv(3).astype(jnp.bfloat16)
    kw_ref[...] = kv(4).astype(jnp.bfloat16)
    vw_ref[...] = kv(5).astype(jnp.bfloat16)
    gate_ref[...] = jax.nn.sigmoid(proj[:, C_GATE:C_GATE + LANES])

    cb = proj[:, C_CONV:C_CONV + CONV_WIDTH]
    cc = proj[:, C_CONV + CONV_WIDTH:C_CONV + 2 * CONV_WIDTH]
    ch = proj[:, C_CONV + 2 * CONV_WIDTH:C_CONV + 3 * CONV_WIDTH]
    z = cc * ch
    tm = z.shape[0]

    @pl.when(i % tiles_per_seq == 0)
    def _():
        carry_ref[...] = jnp.zeros_like(carry_ref)

    prev1 = carry_ref[SUBLANES - 1:SUBLANES, :]
    prev2 = carry_ref[SUBLANES - 2:SUBLANES - 1, :]
    row = lax.broadcasted_iota(jnp.int32, (tm, CONV_WIDTH), 0)
    z1 = jnp.where(row == 0, prev1, pltpu.roll(z, 1, 0))
    z2 = pltpu.roll(z, 2, 0)
    z2 = jnp.where(row == 0, prev2, jnp.where(row == 1, prev1, z2))
    carry_ref[...] = z[tm - SUBLANES:, :]
    y = cw_ref[0:1, :] * z2 + cw_ref[1:2, :] * z1 + cw_ref[2:3, :] * z
    conv_ref[...] = (_rms(cb * y, CONV_WIDTH) * cg_ref[...]).astype(jnp.bfloat16)


def _in_projection(x2, mod3, ln_g, w_perm, conv_w, conv_g, seq):
    n = x2.shape[0]
    tm = TM_PROJ
    tps = seq // tm
    row = lambda i: (i, 0)
    kv_f32 = jax.ShapeDtypeStruct((n, KV_W), jnp.float32)
    kv_bf = jax.ShapeDtypeStruct((n, KV_W), jnp.bfloat16)
    return pl.pallas_call(
        functools.partial(_inproj_kernel, tps),
        grid=(n // tm,),
        in_specs=[pl.BlockSpec((tm, D_MODEL), row),
                  pl.BlockSpec((1, 6, D_MODEL), lambda i: (i // tps, 0, 0)),
                  pl.BlockSpec((1, D_MODEL), lambda i: (0, 0)),
                  pl.BlockSpec((D_MODEL, N_COLS), lambda i: (0, 0)),
                  pl.BlockSpec((CONV_K, CONV_WIDTH), lambda i: (0, 0)),
                  pl.BlockSpec((1, CONV_WIDTH), lambda i: (0, 0))],
        out_specs=[pl.BlockSpec((NSA_REP, tm, LANES), lambda i: (0, i, 0))]
                  + [pl.BlockSpec((tm, KV_W), row)] * 6
                  + [pl.BlockSpec((tm, LANES), row), pl.BlockSpec((tm, CONV_WIDTH), row)],
        out_shape=[jax.ShapeDtypeStruct((NSA_REP, n, LANES), jnp.bfloat16),
                   kv_f32, kv_f32, kv_bf, kv_bf, kv_bf, kv_bf,
                   jax.ShapeDtypeStruct((n, LANES), jnp.float32),
                   jax.ShapeDtypeStruct((n, CONV_WIDTH), jnp.bfloat16)],
        scratch_shapes=[pltpu.VMEM((SUBLANES, CONV_WIDTH), jnp.float32)],
        compiler_params=_params(("arbitrary",)),
        name="in_projection",
    )(x2, mod3, ln_g, w_perm, conv_w, conv_g)


def _compress_kernel(tok_ref, pe_ref, w1_ref, w2_ref, o_ref):
    half = (CMP_BLK // 2) * HEAD_DIM
    w1a = w1_ref[0, 0:half, :]
    w1b = w1_ref[0, half:2 * half, :]
    pe_term = _dot(pe_ref[0], w1_ref[0], HIGHEST)
    ncp = tok_ref.shape[3]
    row = lax.broadcasted_iota(jnp.int32, (ncp, HEAD_DIM), 0)
    for g in range(NSA_KV_HEADS):
        t = tok_ref[0, 0, g]
        a = _dot(t, w1a, HIGHEST)
        b = _dot(t, w1b, HIGHEST)
        pre = a + pltpu.roll(b, ncp - 1, 0) + pe_term
        out = _dot(jax.nn.gelu(pre), w2_ref[0], HIGHEST)
        o_ref[0, 0, :, g * HEAD_DIM:(g + 1) * HEAD_DIM] = jnp.where(row < ncp - 1, out, 0.0)


def _compress(tok, pe, w1, w2):
    _, B, G, ncp, width = tok.shape
    return pl.pallas_call(
        _compress_kernel,
        grid=(2, B),
        in_specs=[pl.BlockSpec((1, 1, G, ncp, width), lambda k, b: (k, b, 0, 0, 0)),
                  pl.BlockSpec((1, 1, CMP_BLK * HEAD_DIM), lambda k, b: (k, 0, 0)),
                  pl.BlockSpec((1, CMP_BLK * HEAD_DIM, HEAD_DIM), lambda k, b: (k, 0, 0)),
                  pl.BlockSpec((1, HEAD_DIM, HEAD_DIM), lambda k, b: (k, 0, 0))],
        out_specs=pl.BlockSpec((1, 1, ncp, KV_W), lambda k, b: (k, b, 0, 0)),
        out_shape=jax.ShapeDtypeStruct((2, B, ncp, KV_W), jnp.float32),
        compiler_params=_params(("arbitrary", "arbitrary")),
        name="compress_kv",
    )(tok, pe, w1, w2)


M_FLOOR = -1e29


def _hi_lo(x):
    hi = x.astype(jnp.bfloat16)
    return hi, (x - hi.astype(jnp.float32)).astype(jnp.bfloat16)


def _attn_kernel(n_slc, q_ref, ks_ref, vs_ref, kw_ref, vw_ref, cmp_ref, gate_ref,
                 bcmp_ref, bnear_ref, c2s_ref, ng_ref, o_ref, og0_ref):
    qi = pl.program_id(1)
    t0 = qi * TQ
    rows = NSA_REP * TQ
    ncp = cmp_ref.shape[2]
    q4 = q_ref[...].reshape(rows, LANES)
    lane = lax.broadcasted_iota(jnp.int32, (1, LANES), 1)
    tq_col = t0 + lax.broadcasted_iota(jnp.int32, (TQ, 1), 0)
    t_col = jnp.concatenate([tq_col] * NSA_REP, axis=0)
    rowq = lax.broadcasted_iota(jnp.int32, (TQ, 1), 0)
    colk = lax.broadcasted_iota(jnp.int32, (1, TK), 1)
    gates_hi, gates_lo = _hi_lo(gate_ref[...])
    n_near = bnear_ref.shape[1] - 1

    for g in range(NSA_KV_HEADS):
        in_g = (lane // HEAD_DIM) == g
        q = jnp.where(in_g, q4, jnp.zeros_like(q4))

        kc_hi, kc_lo = _hi_lo(cmp_ref[0, 0])
        s = _dot_nt(q, kc_hi) + _dot_nt(q, kc_lo) + bcmp_ref[g, 0]
        n_idx = lax.broadcasted_iota(jnp.int32, (1, ncp), 1)
        valid = (n_idx * CMP_STRIDE + (CMP_BLK - 1) <= t_col) & (n_idx < ncp - 1)
        s = jnp.where(valid, s, NEG)
        e = jnp.where(valid, jnp.exp(s - jnp.max(s, axis=-1, keepdims=True)), 0.0)
        e_hi, e_lo = _hi_lo(e)
        ones = jnp.ones((ncp, ncp), jnp.bfloat16)
        den = _dot(e_hi, ones) + _dot(e_lo, ones)
        p_cmp = e / jnp.maximum(den, 1e-30)
        o_cmp = _dot(p_cmp.astype(jnp.bfloat16), cmp_ref[1, 0].astype(jnp.bfloat16))

        p_sum = p_cmp[0:TQ]
        for r in range(1, NSA_REP):
            p_sum = p_sum + p_cmp[r * TQ:(r + 1) * TQ]
        ps_hi, ps_lo = _hi_lo(p_sum)
        imp = (_dot_nt(c2s_ref[...], ps_hi) + _dot_nt(c2s_ref[...], ps_lo))[0:n_slc]
        blk = lax.broadcasted_iota(jnp.int32, (n_slc, TQ), 0)
        blk_f = blk.astype(jnp.float32)
        tok = t0 + lax.broadcasted_iota(jnp.int32, (1, TQ), 1)
        cur = tok // SEL_BLK
        forced = (blk == 0) | (blk == cur) | (blk == cur - 1)
        score = jnp.where(forced, FORCE_SCORE, jnp.where(blk * SEL_BLK <= tok, imp, -jnp.inf))
        taken = jnp.zeros((n_slc, TQ), jnp.bool_)
        for _ in range(min(SEL_TOPN, n_slc)):
            eff = jnp.where(taken, -jnp.inf, score)
            mx = jnp.max(eff, axis=0, keepdims=True)
            cand = jnp.where((eff == mx) & jnp.logical_not(taken), blk_f, float(LANES))
            taken = taken | (blk_f == jnp.min(cand, axis=0, keepdims=True))
        sel_t = jnp.concatenate([jnp.where(taken, 1.0, 0.0),
                                 jnp.zeros((LANES - n_slc, TQ), jnp.float32)], axis=0)
        sel = sel_t.T.astype(jnp.bfloat16)
        blk_row = lax.broadcasted_iota(jnp.int32, (LANES, TK), 0)
        blk_col = lax.broadcasted_iota(jnp.int32, (LANES, TK), 1) // SEL_BLK

        causal_add = jnp.where(rowq - colk >= 0, 0.0, NEG)
        edge_add = jnp.where(rowq - colk < 0, 0.0, NEG)

        def step(k_ref, v_ref, kt, mask_add, carry):
            m_old, acc_old = carry
            at = pl.ds(pl.multiple_of(kt * TK, TK), TK)
            s = _dot_nt(q, k_ref[at, :]) + bnear_ref[g, jnp.minimum(qi - kt, n_near)]
            s = (s.reshape(NSA_REP, TQ, TK) + mask_add[None]).reshape(rows, TK)
            m_new = jnp.maximum(m_old, jnp.max(s, axis=-1, keepdims=True))
            p = jnp.exp(s - m_new).astype(jnp.bfloat16)
            v_aug = jnp.where(in_g, v_ref[at, :], jnp.ones((TK, LANES), jnp.bfloat16))
            return m_new, jnp.exp(m_old - m_new) * acc_old + _dot(p, v_aug)

        def slc_step(kt, carry):
            expand = jnp.where(blk_row == blk_col + kt * (TK // SEL_BLK), 1.0, 0.0)
            chosen = _dot(sel, expand.astype(jnp.bfloat16))
            mask_add = (chosen - 1.0) * (-NEG) + jnp.where(kt == qi, causal_add, 0.0)
            return step(ks_ref, vs_ref, kt, mask_add, carry)

        def win_step(kt, carry):
            mask_add = jnp.where(kt == qi, causal_add,
                                 jnp.where(kt == qi - WINDOW // TK, edge_add, 0.0))
            return step(kw_ref, vw_ref, kt, mask_add, carry)

        def finish(acc):
            return acc / jnp.maximum(pltpu.roll(acc, HEAD_DIM, 1), 1e-30)

        init = (jnp.full((rows, 1), M_FLOOR, jnp.float32), jnp.zeros((rows, LANES), jnp.float32))
        o_slc = finish(lax.fori_loop(0, qi + 1, slc_step, init)[1])
        o_win = finish(lax.fori_loop(jnp.maximum(qi - WINDOW // TK, 0), qi + 1, win_step, init)[1])

        n_gate = NSA_REP * 3
        src = lax.broadcasted_iota(jnp.int32, (LANES, n_gate * LANES), 0)
        dst = lax.broadcasted_iota(jnp.int32, (LANES, n_gate * LANES), 1) // LANES
        spread = jnp.where(src == g * n_gate + dst, 1.0, 0.0).astype(jnp.bfloat16)
        gate_x = _dot(gates_hi, spread) + _dot(gates_lo, spread)
        for r in range(NSA_REP):
            sl = slice(r * TQ, (r + 1) * TQ)
            gate = lambda j: gate_x[:, (r * 3 + j) * LANES:(r * 3 + j + 1) * LANES]
            o = gate(0) * o_cmp[sl] + gate(1) * o_slc[sl] + gate(2) * o_win[sl]
            if g == 0:
                og0_ref[r] = o
            else:
                og0_ref[r] = jnp.where(in_g, o, og0_ref[r])

    ss = jnp.zeros((TQ, 1), jnp.float32)
    for r in range(NSA_REP):
        o = og0_ref[r]
        ss = ss + jnp.sum(o * o, axis=-1, keepdims=True)
    inv = lax.rsqrt(ss * (1.0 / NSA_WIDTH) + EPS)
    for r in range(NSA_REP):
        o_ref[r] = (og0_ref[r] * inv * ng_ref[r]).astype(jnp.bfloat16)


def _attention(q4, ks, vs, kw, vw, cmp_kv, gates, bias_cmp, bias_near, c2s, norm_g, batch, seq):
    n = batch * seq
    nq = seq // TQ
    ncp = cmp_kv.shape[2]
    n_slc = seq // SEL_BLK
    kv_spec = pl.BlockSpec((seq, KV_W), lambda b, i: (b, 0))
    return pl.pallas_call(
        functools.partial(_attn_kernel, n_slc),
        grid=(batch, nq),
        in_specs=[pl.BlockSpec((NSA_REP, TQ, LANES), lambda b, i: (0, b * nq + i, 0)),
                  kv_spec, kv_spec, kv_spec, kv_spec,
                  pl.BlockSpec((2, 1, ncp, KV_W), lambda b, i: (0, b, 0, 0)),
                  pl.BlockSpec((TQ, LANES), lambda b, i: (b * nq + i, 0)),
                  pl.BlockSpec((NSA_KV_HEADS, 1, NSA_REP * TQ, ncp), lambda b, i: (0, i, 0, 0)),
                  pl.BlockSpec(bias_near.shape, lambda b, i: (0, 0, 0, 0)),
                  pl.BlockSpec(c2s.shape, lambda b, i: (0, 0)),
                  pl.BlockSpec((NSA_REP, 1, LANES), lambda b, i: (0, 0, 0))],
        out_specs=pl.BlockSpec((NSA_REP, TQ, LANES), lambda b, i: (0, b * nq + i, 0)),
        out_shape=jax.ShapeDtypeStruct((NSA_REP, n, LANES), jnp.bfloat16),
        scratch_shapes=[pltpu.VMEM((NSA_REP, TQ, LANES), jnp.float32)],
        compiler_params=_params(("arbitrary", "arbitrary")),
        name="sparse_attention",
    )(q4, ks, vs, kw, vw, cmp_kv, gates, bias_cmp, bias_near, c2s, norm_g)


def _topk_rows(s, k):
    nrow = s.shape[0]
    rowf = lax.broadcasted_iota(jnp.int32, s.shape, 0).astype(jnp.float32)
    slot = lax.broadcasted_iota(jnp.int32, (k, s.shape[1]), 0)
    vals = jnp.zeros((k, s.shape[1]), jnp.float32)
    idxs = jnp.zeros((k, s.shape[1]), jnp.float32)
    for j in range(k):
        m = jnp.max(s, axis=0, keepdims=True)
        first = jnp.min(jnp.where(s == m, rowf, float(nrow)), axis=0, keepdims=True)
        vals = jnp.where(slot == j, m, vals)
        idxs = jnp.where(slot == j, first, idxs)
        s = jnp.where(rowf == first, -jnp.inf, s)
    return vals, idxs


def _route_kernel(attn_ref, conv_ref, x_ref, mod_ref, wout_ref, g_ref, wq_ref, keys_ref,
                  x1_ref, h2_ref, eid_ref, gate_ref):
    mixed = jnp.concatenate([attn_ref[r] for r in range(NSA_REP)] + [conv_ref[...]], axis=-1)
    x1 = x_ref[...] + mod_ref[0, 2:3, :] * _dot(mixed, wout_ref[...])
    x1_ref[...] = x1
    h2 = (_rms(x1, D_MODEL) * g_ref[...]) * (1.0 + mod_ref[0, 4:5, :]) + mod_ref[0, 3:4, :]
    h2_ref[...] = h2
    qp = _dot(h2.astype(jnp.bfloat16), wq_ref[...]).astype(jnp.bfloat16)

    tm = qp.shape[0]
    slot = lax.broadcasted_iota(jnp.int32, (PEER_TOPK, tm), 0)
    K = PEER_TOPK
    n_mid = K // 2 - 1
    row = lax.broadcasted_iota(jnp.int32, (K + SUBLANES * n_mid + SUBLANES, tm), 0)
    mid_a = (row - K) // SUBLANES + 1
    tail = row >= K + SUBLANES * n_mid
    cand_a = jnp.where(row < K, 0, jnp.where(tail, K // 2 + row % SUBLANES, mid_a))
    cand_b = jnp.where(row < K, row, jnp.where(tail, 0, row % SUBLANES))
    in_stair = (cand_a + 1) * (cand_b + 1) <= K
    rowf = (cand_a * K + cand_b).astype(jnp.float32)

    def staircase(first_half, second_half):
        parts = [first_half[0:1] + second_half]
        parts += [first_half[a:a + 1] + second_half[0:SUBLANES] for a in range(1, n_mid + 1)]
        parts += [first_half[K // 2:K] + second_half[0:1]]
        return jnp.concatenate(parts, axis=0)

    for h in range(PEER_HEADS):
        blk = qp[:, h * PEER_QDIM:(h + 1) * PEER_QDIM]
        v1, i1 = _topk_rows(_dot_nt(keys_ref[2 * h], blk), PEER_TOPK)
        v2, i2 = _topk_rows(_dot_nt(keys_ref[2 * h + 1], blk), PEER_TOPK)
        cand = jnp.where(in_stair, staircase(v1, v2), -jnp.inf)
        eid = staircase(i1 * float(PEER_NKEYS), i2)
        top_s = jnp.zeros((PEER_TOPK, tm), jnp.float32)
        top_e = jnp.zeros((PEER_TOPK, tm), jnp.float32)
        for j in range(PEER_TOPK):
            m = jnp.max(cand, axis=0, keepdims=True)
            first = jnp.min(jnp.where(cand == m, rowf, 1e9), axis=0, keepdims=True)
            hit = rowf == first
            e = jnp.sum(jnp.where(hit, eid, 0.0), axis=0, keepdims=True)
            top_s = jnp.where(slot == j, m, top_s)
            top_e = jnp.where(slot == j, e, top_e)
            cand = jnp.where(hit, -jnp.inf, cand)
        ex = jnp.exp(top_s - jnp.max(top_s, axis=0, keepdims=True))
        gate_ref[h * PEER_TOPK:(h + 1) * PEER_TOPK, :] = ex / jnp.sum(ex, axis=0, keepdims=True)
        eid_ref[h * PEER_TOPK:(h + 1) * PEER_TOPK, :] = top_e.astype(jnp.int32)


def _out_proj_route(attn4, conv_n, x2, mod3, w_out_p, ln_g, wq, keys_x, seq):
    n = x2.shape[0]
    tm = TM_OUT
    tps = seq // tm
    row = lambda i: (i, 0)
    const2 = lambda i: (0, 0)
    return pl.pallas_call(
        _route_kernel,
        grid=(n // tm,),
        in_specs=[pl.BlockSpec((NSA_REP, tm, LANES), lambda i: (0, i, 0)),
                  pl.BlockSpec((tm, CONV_WIDTH), row),
                  pl.BlockSpec((tm, D_MODEL), row),
                  pl.BlockSpec((1, 6, D_MODEL), lambda i: (i // tps, 0, 0)),
                  pl.BlockSpec((D_MODEL, D_MODEL), const2),
                  pl.BlockSpec((1, D_MODEL), const2),
                  pl.BlockSpec((D_MODEL, PEER_HEADS * PEER_QDIM), const2),
                  pl.BlockSpec((2 * PEER_HEADS, PEER_NKEYS, PEER_QDIM), lambda i: (0, 0, 0))],
        out_specs=[pl.BlockSpec((tm, D_MODEL), row), pl.BlockSpec((tm, D_MODEL), row),
                   pl.BlockSpec((PEER_E, tm), lambda i: (0, i)),
                   pl.BlockSpec((PEER_E, tm), lambda i: (0, i))],
        out_shape=[jax.ShapeDtypeStruct((n, D_MODEL), jnp.float32),
                   jax.ShapeDtypeStruct((n, D_MODEL), jnp.float32),
                   jax.ShapeDtypeStruct((PEER_E, n), jnp.int32),
                   jax.ShapeDtypeStruct((PEER_E, n), jnp.float32)],
        compiler_params=_params(("arbitrary",)),
        name="out_proj_route",
    )(attn4, conv_n, x2, mod3, w_out_p, ln_g, wq, keys_x)


ROWS_PER_EXPERT = ROW_CHUNKS // 2


IDX_TILE = TT * PEER_E
TOKENS_PER_TRIP = 4


def _index_copy(idx_hbm, idx_smem, sem, tile, slot):
    dst = idx_smem.at[pl.ds(pl.multiple_of(slot * IDX_TILE, IDX_TILE), IDX_TILE)]
    return pltpu.make_async_copy(idx_hbm.at[tile], dst, sem.at[slot])


def _fetch_indices(idx_hbm, idx_smem, sem):
    i = pl.program_id(0)
    slot = i % 2

    @pl.when(i == 0)
    def _():
        _index_copy(idx_hbm, idx_smem, sem, 0, 0).start()

    _index_copy(idx_hbm, idx_smem, sem, i, slot).wait()

    @pl.when(i + 1 < pl.num_programs(0))
    def _():
        _index_copy(idx_hbm, idx_smem, sem, i + 1, 1 - slot).start()

    return slot


def _gather_rows(slot, t, idx_smem, tab_ref, rows_ref):
    tok_idx = idx_smem.at[pl.ds(pl.multiple_of(slot * IDX_TILE + t * PEER_E, PEER_E), PEER_E)]
    for k in range(PEER_E):
        e = tok_idx[k]
        rows_ref[k * ROWS_PER_EXPERT:(k + 1) * ROWS_PER_EXPERT, :] = tab_ref[e]


def _token_pairs(gather, compute, rows_a, rows_b):
    gather(0, rows_a)
    gather(1, rows_b)

    def group(i, _):
        for j in range(TOKENS_PER_TRIP):
            t = TOKENS_PER_TRIP * i + j
            buf = rows_a if j % 2 == 0 else rows_b
            compute(t, buf)
            gather(jnp.minimum(t + 2, TT - 1), buf)
        return 0

    lax.fori_loop(0, TT // TOKENS_PER_TRIP, group, 0)


def _split_bf16(x):
    hi = x.astype(jnp.bfloat16).astype(jnp.float32)
    return jnp.concatenate([hi, x - hi], axis=0).astype(jnp.bfloat16)


def _expert_of_row():
    n = lax.broadcasted_iota(jnp.int32, (PEER_E * ROW_CHUNKS, PEER_E), 0) // ROW_CHUNKS
    k = lax.broadcasted_iota(jnp.int32, (PEER_E * ROW_CHUNKS, PEER_E), 1)
    return jnp.where(n == k, 1.0, 0.0)


def _peer_act_kernel(idx_hbm, x_ref, gate_ref, tab_ref, w_ref, idx_smem, sem, rows_a, rows_b, z_ref):
    slot = _fetch_indices(idx_hbm, idx_smem, sem)
    m_i = lax.broadcasted_iota(jnp.int32, (2 * ROW_CHUNKS, PEER_E * ROW_CHUNKS), 0) % ROW_CHUNKS
    n_i = lax.broadcasted_iota(jnp.int32, (2 * ROW_CHUNKS, PEER_E * ROW_CHUNKS), 1) % ROW_CHUNKS
    chunk_match = m_i == n_i

    def compute(t, rows_ref):
        x16 = _split_bf16(x_ref[t])
        u = pltpu.bitcast(rows_ref[...], jnp.bfloat16)
        r = _dot_nt(x16, u)
        z_ref[pl.ds(t, 1), :] = jnp.sum(jnp.where(chunk_match, r, 0.0), axis=0, keepdims=True)

    gather = lambda t, rows_ref: _gather_rows(slot, t, idx_smem, tab_ref, rows_ref)
    _token_pairs(gather, compute, rows_a, rows_b)
    act = _dot(z_ref[...], _expert_of_row(), HIGHEST)
    w_ref[...] = gate_ref[...] * jax.nn.gelu(act)


def _peer_out_kernel(idx_hbm, w_ref, tab_ref, o_ref, idx_smem, sem, rows_a, rows_b, w8_ref):
    slot = _fetch_indices(idx_hbm, idx_smem, sem)
    w8_ref[...] = _dot_nt(w_ref[...], _expert_of_row(), HIGHEST)
    m_i = lax.broadcasted_iota(jnp.int32, (ROW_CHUNKS, PEER_E * ROW_CHUNKS), 0)
    n_i = lax.broadcasted_iota(jnp.int32, (ROW_CHUNKS, PEER_E * ROW_CHUNKS), 1) % ROW_CHUNKS
    chunk_match = m_i == n_i

    def compute(t, rows_ref):
        lhs = _split_bf16(jnp.where(chunk_match, w8_ref[pl.ds(t, 1), :], 0.0))
        v = pltpu.bitcast(rows_ref[...], jnp.bfloat16)
        o = _dot(lhs, v)
        o_ref[t] = o[0:ROW_CHUNKS] + o[ROW_CHUNKS:2 * ROW_CHUNKS]

    gather = lambda t, rows_ref: _gather_rows(slot, t, idx_smem, tab_ref, rows_ref)
    _token_pairs(gather, compute, rows_a, rows_b)


def _peer_scratch():
    rows = pltpu.VMEM((PEER_E * ROWS_PER_EXPERT, LANES), jnp.int32)
    return [pltpu.SMEM((2 * IDX_TILE,), jnp.int32), pltpu.SemaphoreType.DMA((2,)), rows, rows,
            pltpu.VMEM((TT, PEER_E * ROW_CHUNKS), jnp.float32)]


def _peer_act(idx_tiles, x8, gates, tab_u):
    n = x8.shape[0]
    return pl.pallas_call(
        _peer_act_kernel,
        grid=(n // TT,),
        in_specs=[pl.BlockSpec(memory_space=pl.ANY),
                  pl.BlockSpec((TT, ROW_CHUNKS, LANES), lambda i: (i, 0, 0)),
                  pl.BlockSpec((TT, PEER_E), lambda i: (i, 0)),
                  pl.BlockSpec(tab_u.shape, lambda i: (0, 0, 0))],
        out_specs=pl.BlockSpec((TT, PEER_E), lambda i: (i, 0)),
        out_shape=jax.ShapeDtypeStruct((n, PEER_E), jnp.float32),
        scratch_shapes=_peer_scratch(),
        compiler_params=_params(("arbitrary",)),
        name="peer_activations",
    )(idx_tiles, x8, gates, tab_u)


def _peer_out(idx_tiles, w, tab_v):
    n = w.shape[0]
    return pl.pallas_call(
        _peer_out_kernel,
        grid=(n // TT,),
        in_specs=[pl.BlockSpec(memory_space=pl.ANY),
                  pl.BlockSpec((TT, PEER_E), lambda i: (i, 0)),
                  pl.BlockSpec(tab_v.shape, lambda i: (0, 0, 0))],
        out_specs=pl.BlockSpec((TT, ROW_CHUNKS, LANES), lambda i: (i, 0, 0)),
        out_shape=jax.ShapeDtypeStruct((n, ROW_CHUNKS, LANES), jnp.float32),
        scratch_shapes=_peer_scratch(),
        compiler_params=_params(("arbitrary",)),
        name="peer_combine",
    )(idx_tiles, w, tab_v)


def _pack_table(tab):
    bits = lax.bitcast_convert_type(tab.astype(jnp.bfloat16), jnp.uint16).astype(jnp.uint32)
    bits = bits.reshape(tab.shape[0], ROWS_PER_EXPERT, 2, LANES)
    words = bits[:, :, 0, :] | (bits[:, :, 1, :] << 16)
    return lax.bitcast_convert_type(words, jnp.int32)


def _final_kernel(x1_ref, p_ref, mod_ref, g_ref, o_ref):
    x2 = x1_ref[...] + mod_ref[0, 5:6, :] * p_ref[...]
    o_ref[...] = _rms(x2, D_MODEL) * g_ref[...]


def _final(x1, peer, mod3, g, seq):
    n = x1.shape[0]
    tm = TM_PROJ
    tps = seq // tm
    row = lambda i: (i, 0)
    return pl.pallas_call(
        _final_kernel,
        grid=(n // tm,),
        in_specs=[pl.BlockSpec((tm, D_MODEL), row), pl.BlockSpec((tm, D_MODEL), row),
                  pl.BlockSpec((1, 6, D_MODEL), lambda i: (i // tps, 0, 0)),
                  pl.BlockSpec((1, D_MODEL), lambda i: (0, 0))],
        out_specs=pl.BlockSpec((tm, D_MODEL), row),
        out_shape=jax.ShapeDtypeStruct((n, D_MODEL), jnp.float32),
        compiler_params=_params(("arbitrary",)),
        name="final_norm",
    )(x1, peer, mod3, g)


def _attn_perm():
    idx = []
    for r in range(NSA_REP):
        for g in range(NSA_KV_HEADS):
            h = g * NSA_REP + r
            idx.extend(range(h * HEAD_DIM, (h + 1) * HEAD_DIM))
    return np.asarray(idx, np.int32)


def _rel_bucket_np(dist):
    n = np.maximum(dist, 0)
    exact = REL_BUCKETS // 2
    log_ratio = (np.log(np.maximum(n, 1).astype(np.float32) / np.float32(exact))
                 / np.float32(math.log(REL_MAX_DIST / exact)))
    large = exact + (log_ratio * np.float32(REL_BUCKETS - exact)).astype(np.int32)
    return np.where(n < exact, n, np.minimum(large, REL_BUCKETS - 1)).astype(np.int32)


def _bias_tables(rel_table, seq):
    nq = seq // TQ
    ncp = seq // CMP_STRIDE

    def lookup(bucket):
        onehot = (jnp.asarray(bucket)[..., None] == jnp.arange(REL_BUCKETS)).astype(jnp.float32)
        out = jnp.einsum("...b,bh->...h", onehot, rel_table, precision=HIGHEST)
        return out.reshape(bucket.shape + (NSA_KV_HEADS, NSA_REP))
    n_near = -(-REL_MAX_DIST // TQ) + 1
    r_i = np.arange(TQ)[:, None]
    c_i = np.arange(TK)[None, :]
    near = np.stack([_rel_bucket_np(d * TQ + r_i - c_i) for d in range(n_near)]
                    + [np.full((TQ, TK), REL_BUCKETS - 1, np.int32)])
    b_near = jnp.transpose(lookup(near), (3, 0, 4, 1, 2))
    b_near = b_near.reshape(NSA_KV_HEADS, n_near + 1, NSA_REP * TQ, TK)
    t_i = np.arange(seq)[:, None]
    n_i = np.arange(ncp)[None, :]
    cmp_b = _rel_bucket_np(t_i - (n_i * CMP_STRIDE + CMP_BLK - 1))
    b_cmp = jnp.transpose(lookup(cmp_b), (2, 3, 0, 1))
    b_cmp = b_cmp.reshape(NSA_KV_HEADS, NSA_REP, nq, TQ, ncp).transpose(0, 2, 1, 3, 4)
    b_cmp = b_cmp.reshape(NSA_KV_HEADS, nq, NSA_REP * TQ, ncp)
    return b_cmp, b_near


def _cmp_to_slc(seq):
    ncp = seq // CMP_STRIDE
    n_cmp = (seq - CMP_BLK) // CMP_STRIDE + 1
    n_slc = seq // SEL_BLK
    cs = np.arange(ncp) * CMP_STRIDE
    ss = np.arange(LANES) * SEL_BLK
    ov = (cs[:, None] < ss[None, :] + SEL_BLK) & (cs[:, None] + CMP_BLK > ss[None, :])
    ov &= (np.arange(ncp)[:, None] < n_cmp) & (np.arange(LANES)[None, :] < n_slc)
    return jnp.asarray(ov.T, jnp.bfloat16)


def kernel(x, c, ln_mix_g, ln_ffn_g, w_mod, b_mod, w_in, cmp_pe_k, cmp_pe_v, cmp_wk1, cmp_wk2,
           cmp_wv1, cmp_wv2, conv_w, norm_attn_g, norm_conv_g, w_out, peer_wq, peer_keys, peer_u,
           peer_v, rel_table, ln_final_g):
    B, S, _ = x.shape
    n = B * S
    assert w_mod.shape[0] == 1 and S % TM_PROJ == 0 and S % TQ == 0 and n % TT == 0
    assert TQ == TK and WINDOW % TK == 0 and TT % TOKENS_PER_TRIP == 0
    x2 = x.reshape(n, D_MODEL)
    perm = _attn_perm()

    mod3 = _modulation(c, w_mod[0], b_mod[0]).reshape(B, 6, D_MODEL)

    w = w_in[0]
    q_cols = w[:, :NSA_WIDTH][:, perm]
    kv_cols = w[:, NSA_WIDTH:NSA_WIDTH + 6 * KV_W]
    g0 = NSA_WIDTH + 6 * KV_W
    gate_cols = jnp.pad(w[:, g0:g0 + NSA_HEADS * 3], ((0, 0), (0, LANES - NSA_HEADS * 3)))
    conv_cols = w[:, g0 + NSA_HEADS * 3:]
    w_perm = jnp.concatenate([q_cols, kv_cols, gate_cols, conv_cols], axis=1).astype(jnp.bfloat16)

    q4, kc, vc, ks, vs, kw, vw, gates, conv_n = _in_projection(
        x2, mod3, ln_mix_g[0].reshape(1, D_MODEL), w_perm, conv_w[0].reshape(CONV_K, CONV_WIDTH),
        norm_conv_g[0].reshape(1, CONV_WIDTH), S)

    ncp = S // CMP_STRIDE
    tok = jnp.stack([kc, vc]).reshape(2, B, ncp, CMP_STRIDE, NSA_KV_HEADS, HEAD_DIM)
    tok = tok.transpose(0, 1, 4, 2, 3, 5).reshape(2, B, NSA_KV_HEADS, ncp, CMP_STRIDE * HEAD_DIM)
    cmp_kv = _compress(tok,
                       jnp.stack([cmp_pe_k[0], cmp_pe_v[0]]).reshape(2, 1, CMP_BLK * HEAD_DIM),
                       jnp.stack([cmp_wk1[0], cmp_wv1[0]]), jnp.stack([cmp_wk2[0], cmp_wv2[0]]))

    b_cmp, b_near = _bias_tables(rel_table, S)
    attn4 = _attention(q4, ks, vs, kw, vw, cmp_kv, gates, b_cmp, b_near, _cmp_to_slc(S),
                       norm_attn_g[0][perm].reshape(NSA_REP, 1, LANES), B, S)

    w_out_p = jnp.concatenate([w_out[0][:NSA_WIDTH][perm], w_out[0][NSA_WIDTH:]], axis=0)
    half = PEER_QDIM // 2
    keys = peer_keys[0].reshape(2 * PEER_HEADS, PEER_NKEYS, half)
    lo = jnp.pad(keys, ((0, 0), (0, 0), (0, half)))
    hi = jnp.pad(keys, ((0, 0), (0, 0), (half, 0)))
    is_hi = (jnp.arange(2 * PEER_HEADS) % 2 == 1)[:, None, None]
    keys_x = jnp.where(is_hi, hi, lo).astype(jnp.bfloat16)
    x1, h2, eid_t, gate_t = _out_proj_route(
        attn4, conv_n, x2, mod3, w_out_p.astype(jnp.bfloat16), ln_ffn_g[0].reshape(1, D_MODEL),
        peer_wq[0].astype(jnp.bfloat16), keys_x, S)

    idx_tiles = eid_t.T.reshape(n // TT, TT * PEER_E)
    w_act = _peer_act(idx_tiles, h2.reshape(n, ROW_CHUNKS, LANES), gate_t.T, _pack_table(peer_u[0]))
    peer = _peer_out(idx_tiles, w_act, _pack_table(peer_v[0])).reshape(n, D_MODEL)

    out = _final(x1, peer, mod3, ln_final_g.reshape(1, D_MODEL), S)
    return out.reshape(B, S, D_MODEL)
```

```python
import functools
import math

import numpy as np
import jax
import jax.numpy as jnp
from jax import lax
from jax.experimental import pallas as pl
from jax.experimental.pallas import tpu as pltpu

D_MODEL = 1024
HEAD_DIM = 64
NSA_HEADS = 8
NSA_KV_HEADS = 2
NSA_REP = NSA_HEADS // NSA_KV_HEADS
NSA_WIDTH = NSA_HEADS * HEAD_DIM
KV_W = NSA_KV_HEADS * HEAD_DIM
CONV_WIDTH = D_MODEL - NSA_WIDTH
CONV_K = 3
CMP_BLK = 32
CMP_STRIDE = 16
SEL_BLK = 64
SEL_TOPN = 8
WINDOW = 512
FORCE_SCORE = 1e4
REL_BUCKETS = 32
REL_MAX_DIST = 128
PEER_HEADS = 8
PEER_NKEYS = 128
PEER_EXPERTS = PEER_NKEYS * PEER_NKEYS
PEER_QDIM = 128
PEER_TOPK = 16
PEER_E = PEER_HEADS * PEER_TOPK
EPS = 1e-6

LANES = 128
SUBLANES = 8
ROW_CHUNKS = D_MODEL // LANES
VMEM_LIMIT = 56 * 1024 * 1024

TM_PROJ = 512
TM_OUT = 256
TQ = 256
TK = 256
TT = 64
NEG = -1e30
HIGHEST = lax.Precision.HIGHEST

C_Q = 0
C_KV = C_Q + NSA_WIDTH
C_GATE = C_KV + 6 * KV_W
C_CONV = C_GATE + LANES
N_COLS = C_CONV + 3 * CONV_WIDTH


def _params(sem):
    return pltpu.CompilerParams(dimension_semantics=sem, vmem_limit_bytes=VMEM_LIMIT)


def _dot(a, b, precision=None):
    return jnp.dot(a, b, preferred_element_type=jnp.float32, precision=precision)


def _dot_nt(a, b, precision=None):
    return lax.dot_general(a, b, (((1,), (1,)), ((), ())),
                           preferred_element_type=jnp.float32, precision=precision)


def _rms(x, n):
    return x * lax.rsqrt(jnp.sum(x * x, axis=-1, keepdims=True) * (1.0 / n) + EPS)


def _mod_kernel(c_ref, w_ref, b_ref, o_ref):
    c = c_ref[...]
    act = c * jax.nn.sigmoid(c)
    o_ref[...] = _dot(act, w_ref[...], HIGHEST) + b_ref[...]


def _modulation(c, w_mod, b_mod):
    B = c.shape[0]
    n = w_mod.shape[1]
    bn = D_MODEL
    return pl.pallas_call(
        _mod_kernel,
        grid=(n // bn,),
        in_specs=[pl.BlockSpec((B, D_MODEL), lambda j: (0, 0)),
                  pl.BlockSpec((D_MODEL, bn), lambda j: (0, j)),
                  pl.BlockSpec((1, bn), lambda j: (0, j))],
        out_specs=pl.BlockSpec((B, bn), lambda j: (0, j)),
        out_shape=jax.ShapeDtypeStruct((B, n), jnp.float32),
        compiler_params=_params(("arbitrary",)),
        name="modulation",
    )(c, w_mod, b_mod.reshape(1, n))


def _inproj_kernel(tiles_per_seq, x_ref, mod_ref, g_ref, w_ref, cw_ref, cg_ref,
                   q_ref, kc_ref, vc_ref, ks_ref, vs_ref, kw_ref, vw_ref, gate_ref, conv_ref,
                   carry_ref):
    i = pl.program_id(0)
    x = x_ref[...]
    sh = mod_ref[0, 0:1, :]
    sc = mod_ref[0, 1:2, :]
    h = (_rms(x, D_MODEL) * g_ref[...]) * (1.0 + sc) + sh
    proj = _dot(h.astype(jnp.bfloat16), w_ref[...])

    for r in range(NSA_REP):
        q_ref[r] = (proj[:, C_Q + r * LANES:C_Q + (r + 1) * LANES] * (HEAD_DIM ** -0.5)
                    ).astype(jnp.bfloat16)
    kv = lambda k: proj[:, C_KV + k * KV_W:C_KV + (k + 1) * KV_W]
    kc_ref[...] = kv(0)
    vc_ref[...] = kv(1)
    ks_ref[...] = kv(2).astype(jnp.bfloat16)
    vs_ref[...] = kv(3).astype(jnp.bfloat16)
    kw_ref[...] = kv(4).astype(jnp.bfloat16)
    vw_ref[...] = kv(5).astype(jnp.bfloat16)
    gate_ref[...] = jax.nn.sigmoid(proj[:, C_GATE:C_GATE + LANES])

    cb = proj[:, C_CONV:C_CONV + CONV_WIDTH]
    cc = proj[:, C_CONV + CONV_WIDTH:C_CONV + 2 * CONV_WIDTH]
    ch = proj[:, C_CONV + 2 * CONV_WIDTH:C_CONV + 3 * CONV_WIDTH]
    z = cc * ch
    tm = z.shape[0]

    @pl.when(i % tiles_per_seq == 0)
    def _():
        carry_ref[...] = jnp.zeros_like(carry_ref)

    prev1 = carry_ref[SUBLANES - 1:SUBLANES, :]
    prev2 = carry_ref[SUBLANES - 2:SUBLANES - 1, :]
    row = lax.broadcasted_iota(jnp.int32, (tm, CONV_WIDTH), 0)
    z1 = jnp.where(row == 0, prev1, pltpu.roll(z, 1, 0))
    z2 = pltpu.roll(z, 2, 0)
    z2 = jnp.where(row == 0, prev2, jnp.where(row == 1, prev1, z2))
    carry_ref[...] = z[tm - SUBLANES:, :]
    y = cw_ref[0:1, :] * z2 + cw_ref[1:2, :] * z1 + cw_ref[2:3, :] * z
    conv_ref[...] = (_rms(cb * y, CONV_WIDTH) * cg_ref[...]).astype(jnp.bfloat16)


def _in_projection(x2, mod3, ln_g, w_perm, conv_w, conv_g, seq):
    n = x2.shape[0]
    tm = TM_PROJ
    tps = seq // tm
    row = lambda i: (i, 0)
    kv_f32 = jax.ShapeDtypeStruct((n, KV_W), jnp.float32)
    kv_bf = jax.ShapeDtypeStruct((n, KV_W), jnp.bfloat16)
    return pl.pallas_call(
        functools.partial(_inproj_kernel, tps),
        grid=(n // tm,),
        in_specs=[pl.BlockSpec((tm, D_MODEL), row),
                  pl.BlockSpec((1, 6, D_MODEL), lambda i: (i // tps, 0, 0)),
                  pl.BlockSpec((1, D_MODEL), lambda i: (0, 0)),
                  pl.BlockSpec((D_MODEL, N_COLS), lambda i: (0, 0)),
                  pl.BlockSpec((CONV_K, CONV_WIDTH), lambda i: (0, 0)),
                  pl.BlockSpec((1, CONV_WIDTH), lambda i: (0, 0))],
        out_specs=[pl.BlockSpec((NSA_REP, tm, LANES), lambda i: (0, i, 0))]
                  + [pl.BlockSpec((tm, KV_W), row)] * 6
                  + [pl.BlockSpec((tm, LANES), row), pl.BlockSpec((tm, CONV_WIDTH), row)],
        out_shape=[jax.ShapeDtypeStruct((NSA_REP, n, LANES), jnp.bfloat16),
                   kv_f32, kv_f32, kv_bf, kv_bf, kv_bf, kv_bf,
                   jax.ShapeDtypeStruct((n, LANES), jnp.float32),
                   jax.ShapeDtypeStruct((n, CONV_WIDTH), jnp.bfloat16)],
        scratch_shapes=[pltpu.VMEM((SUBLANES, CONV_WIDTH), jnp.float32)],
        compiler_params=_params(("arbitrary",)),
        name="in_projection",
    )(x2, mod3, ln_g, w_perm, conv_w, conv_g)


def _compress_kernel(tok_ref, pe_ref, w1_ref, w2_ref, o_ref):
    half = (CMP_BLK // 2) * HEAD_DIM
    w1a = w1_ref[0, 0:half, :]
    w1b = w1_ref[0, half:2 * half, :]
    pe_term = _dot(pe_ref[0], w1_ref[0], HIGHEST)
    ncp = tok_ref.shape[3]
    row = lax.broadcasted_iota(jnp.int32, (ncp, HEAD_DIM), 0)
    for g in range(NSA_KV_HEADS):
        t = tok_ref[0, 0, g]
        a = _dot(t, w1a, HIGHEST)
        b = _dot(t, w1b, HIGHEST)
        pre = a + pltpu.roll(b, ncp - 1, 0) + pe_term
        out = _dot(jax.nn.gelu(pre), w2_ref[0], HIGHEST)
        o_ref[0, 0, :, g * HEAD_DIM:(g + 1) * HEAD_DIM] = jnp.where(row < ncp - 1, out, 0.0)


def _compress(tok, pe, w1, w2):
    _, B, G, ncp, width = tok.shape
    return pl.pallas_call(
        _compress_kernel,
        grid=(2, B),
        in_specs=[pl.BlockSpec((1, 1, G, ncp, width), lambda k, b: (k, b, 0, 0, 0)),
                  pl.BlockSpec((1, 1, CMP_BLK * HEAD_DIM), lambda k, b: (k, 0, 0)),
                  pl.BlockSpec((1, CMP_BLK * HEAD_DIM, HEAD_DIM), lambda k, b: (k, 0, 0)),
                  pl.BlockSpec((1, HEAD_DIM, HEAD_DIM), lambda k, b: (k, 0, 0))],
        out_specs=pl.BlockSpec((1, 1, ncp, KV_W), lambda k, b: (k, b, 0, 0)),
        out_shape=jax.ShapeDtypeStruct((2, B, ncp, KV_W), jnp.float32),
        compiler_params=_params(("arbitrary", "arbitrary")),
        name="compress_kv",
    )(tok, pe, w1, w2)


M_FLOOR = -1e29


def _hi_lo(x):
    hi = x.astype(jnp.bfloat16)
    return hi, (x - hi.astype(jnp.float32)).astype(jnp.bfloat16)


def _attn_kernel(n_slc, q_ref, ks_ref, vs_ref, kw_ref, vw_ref, cmp_ref, gate_ref,
                 bcmp_ref, bnear_ref, c2s_ref, ng_ref, o_ref, og0_ref):
    qi = pl.program_id(1)
    t0 = qi * TQ
    rows = NSA_REP * TQ
    ncp = cmp_ref.shape[2]
    q4 = q_ref[...].reshape(rows, LANES)
    lane = lax.broadcasted_iota(jnp.int32, (1, LANES), 1)
    tq_col = t0 + lax.broadcasted_iota(jnp.int32, (TQ, 1), 0)
    t_col = jnp.concatenate([tq_col] * NSA_REP, axis=0)
    rowq = lax.broadcasted_iota(jnp.int32, (TQ, 1), 0)
    colk = lax.broadcasted_iota(jnp.int32, (1, TK), 1)
    gates_hi, gates_lo = _hi_lo(gate_ref[...])
    n_near = bnear_ref.shape[1] - 1

    for g in range(NSA_KV_HEADS):
        in_g = (lane // HEAD_DIM) == g
        q = jnp.where(in_g, q4, jnp.zeros_like(q4))

        kc_hi, kc_lo = _hi_lo(cmp_ref[0, 0])
        s = _dot_nt(q, kc_hi) + _dot_nt(q, kc_lo) + bcmp_ref[g, 0]
        n_idx = lax.broadcasted_iota(jnp.int32, (1, ncp), 1)
        valid = (n_idx * CMP_STRIDE + (CMP_BLK - 1) <= t_col) & (n_idx < ncp - 1)
        s = jnp.where(valid, s, NEG)
        e = jnp.where(valid, jnp.exp(s - jnp.max(s, axis=-1, keepdims=True)), 0.0)
        e_hi, e_lo = _hi_lo(e)
        ones = jnp.ones((ncp, ncp), jnp.bfloat16)
        den = _dot(e_hi, ones) + _dot(e_lo, ones)
        p_cmp = e / jnp.maximum(den, 1e-30)
        o_cmp = _dot(p_cmp.astype(jnp.bfloat16), cmp_ref[1, 0].astype(jnp.bfloat16))

        p_sum = p_cmp[0:TQ]
        for r in range(1, NSA_REP):
            p_sum = p_sum + p_cmp[r * TQ:(r + 1) * TQ]
        ps_hi, ps_lo = _hi_lo(p_sum)
        imp = (_dot_nt(c2s_ref[...], ps_hi) + _dot_nt(c2s_ref[...], ps_lo))[0:n_slc]
        blk = lax.broadcasted_iota(jnp.int32, (n_slc, TQ), 0)
        blk_f = blk.astype(jnp.float32)
        tok = t0 + lax.broadcasted_iota(jnp.int32, (1, TQ), 1)
        cur = tok // SEL_BLK
        forced = (blk == 0) | (blk == cur) | (blk == cur - 1)
        score = jnp.where(forced, FORCE_SCORE, jnp.where(blk * SEL_BLK <= tok, imp, -jnp.inf))
        taken = jnp.zeros((n_slc, TQ), jnp.bool_)
        for _ in range(min(SEL_TOPN, n_slc)):
            eff = jnp.where(taken, -jnp.inf, score)
            mx = jnp.max(eff, axis=0, keepdims=True)
            cand = jnp.where((eff == mx) & jnp.logical_not(taken), blk_f, float(LANES))
            taken = taken | (blk_f == jnp.min(cand, axis=0, keepdims=True))
        sel_t = jnp.concatenate([jnp.where(taken, 1.0, 0.0),
                                 jnp.zeros((LANES - n_slc, TQ), jnp.float32)], axis=0)
        sel = sel_t.T.astype(jnp.bfloat16)
        blk_row = lax.broadcasted_iota(jnp.int32, (LANES, TK), 0)
        blk_col = lax.broadcasted_iota(jnp.int32, (LANES, TK), 1) // SEL_BLK

        causal_add = jnp.where(rowq - colk >= 0, 0.0, NEG)
        edge_add = jnp.where(rowq - colk < 0, 0.0, NEG)

        def step(k_ref, v_ref, kt, mask_add, carry):
            m_old, acc_old = carry
            at = pl.ds(pl.multiple_of(kt * TK, TK), TK)
            s = _dot_nt(q, k_ref[at, :]) + bnear_ref[g, jnp.minimum(qi - kt, n_near)]
            s = (s.reshape(NSA_REP, TQ, TK) + mask_add[None]).reshape(rows, TK)
            m_new = jnp.maximum(m_old, jnp.max(s, axis=-1, keepdims=True))
            p = jnp.exp(s - m_new).astype(jnp.bfloat16)
            v_aug = jnp.where(in_g, v_ref[at, :], jnp.ones((TK, LANES), jnp.bfloat16))
            return m_new, jnp.exp(m_old - m_new) * acc_old + _dot(p, v_aug)

        def slc_step(kt, carry):
            expand = jnp.where(blk_row == blk_col + kt * (TK // SEL_BLK), 1.0, 0.0)
            chosen = _dot(sel, expand.astype(jnp.bfloat16))
            mask_add = (chosen - 1.0) * (-NEG) + jnp.where(kt == qi, causal_add, 0.0)
            return step(ks_ref, vs_ref, kt, mask_add, carry)

        def win_step(kt, carry):
            mask_add = jnp.where(kt == qi, causal_add,
                                 jnp.where(kt == qi - WINDOW // TK, edge_add, 0.0))
            return step(kw_ref, vw_ref, kt, mask_add, carry)

        def finish(acc):
            return acc / jnp.maximum(pltpu.roll(acc, HEAD_DIM, 1), 1e-30)

        init = (jnp.full((rows, 1), M_FLOOR, jnp.float32), jnp.zeros((rows, LANES), jnp.float32))
        o_slc = finish(lax.fori_loop(0, qi + 1, slc_step, init)[1])
        o_win = finish(lax.fori_loop(jnp.maximum(qi - WINDOW // TK, 0), qi + 1, win_step, init)[1])

        n_gate = NSA_REP * 3
        src = lax.broadcasted_iota(jnp.int32, (LANES, n_gate * LANES), 0)
        dst = lax.broadcasted_iota(jnp.int32, (LANES, n_gate * LANES), 1) // LANES
        spread = jnp.where(src == g * n_gate + dst, 1.0, 0.0).astype(jnp.bfloat16)
        gate_x = _dot(gates_hi, spread) + _dot(gates_lo, spread)
        for r in range(NSA_REP):
            sl = slice(r * TQ, (r + 1) * TQ)
            gate = lambda j: gate_x[:, (r * 3 + j) * LANES:(r * 3 + j + 1) * LANES]
            o = gate(0) * o_cmp[sl] + gate(1) * o_slc[sl] + gate(2) * o_win[sl]
            if g == 0:
                og0_ref[r] = o
            else:
                og0_ref[r] = jnp.where(in_g, o, og0_ref[r])

    ss = jnp.zeros((TQ, 1), jnp.float32)
    for r in range(NSA_REP):
        o = og0_ref[r]
        ss = ss + jnp.sum(o * o, axis=-1, keepdims=True)
    inv = lax.rsqrt(ss * (1.0 / NSA_WIDTH) + EPS)
    for r in range(NSA_REP):
        o_ref[r] = (og0_ref[r] * inv * ng_ref[r]).astype(jnp.bfloat16)


def _attention(q4, ks, vs, kw, vw, cmp_kv, gates, bias_cmp, bias_near, c2s, norm_g, batch, seq):
    n = batch * seq
    nq = seq // TQ
    ncp = cmp_kv.shape[2]
    n_slc = seq // SEL_BLK
    kv_spec = pl.BlockSpec((seq, KV_W), lambda b, i: (b, 0))
    return pl.pallas_call(
        functools.partial(_attn_kernel, n_slc),
        grid=(batch, nq),
        in_specs=[pl.BlockSpec((NSA_REP, TQ, LANES), lambda b, i: (0, b * nq + i, 0)),
                  kv_spec, kv_spec, kv_spec, kv_spec,
                  pl.BlockSpec((2, 1, ncp, KV_W), lambda b, i: (0, b, 0, 0)),
                  pl.BlockSpec((TQ, LANES), lambda b, i: (b * nq + i, 0)),
                  pl.BlockSpec((NSA_KV_HEADS, 1, NSA_REP * TQ, ncp), lambda b, i: (0, i, 0, 0)),
                  pl.BlockSpec(bias_near.shape, lambda b, i: (0, 0, 0, 0)),
                  pl.BlockSpec(c2s.shape, lambda b, i: (0, 0)),
                  pl.BlockSpec((NSA_REP, 1, LANES), lambda b, i: (0, 0, 0))],
        out_specs=pl.BlockSpec((NSA_REP, TQ, LANES), lambda b, i: (0, b * nq + i, 0)),
        out_shape=jax.ShapeDtypeStruct((NSA_REP, n, LANES), jnp.bfloat16),
        scratch_shapes=[pltpu.VMEM((NSA_REP, TQ, LANES), jnp.float32)],
        compiler_params=_params(("arbitrary", "arbitrary")),
        name="sparse_attention",
    )(q4, ks, vs, kw, vw, cmp_kv, gates, bias_cmp, bias_near, c2s, norm_g)


def _topk_rows(s, k):
    nrow = s.shape[0]
    rowf = lax.broadcasted_iota(jnp.int32, s.shape, 0).astype(jnp.float32)
    slot = lax.broadcasted_iota(jnp.int32, (k, s.shape[1]), 0)
    vals = jnp.zeros((k, s.shape[1]), jnp.float32)
    idxs = jnp.zeros((k, s.shape[1]), jnp.float32)
    for j in range(k):
        m = jnp.max(s, axis=0, keepdims=True)
        first = jnp.min(jnp.where(s == m, rowf, float(nrow)), axis=0, keepdims=True)
        vals = jnp.where(slot == j, m, vals)
        idxs = jnp.where(slot == j, first, idxs)
        s = jnp.where(rowf == first, -jnp.inf, s)
    return vals, idxs


def _route_kernel(attn_ref, conv_ref, x_ref, mod_ref, wout_ref, g_ref, wq_ref, keys_ref,
                  x1_ref, h2_ref, eid_ref, gate_ref):
    mixed = jnp.concatenate([attn_ref[r] for r in range(NSA_REP)] + [conv_ref[...]], axis=-1)
    x1 = x_ref[...] + mod_ref[0, 2:3, :] * _dot(mixed, wout_ref[...])
    x1_ref[...] = x1
    h2 = (_rms(x1, D_MODEL) * g_ref[...]) * (1.0 + mod_ref[0, 4:5, :]) + mod_ref[0, 3:4, :]
    h2_ref[...] = h2
    qp = _dot(h2.astype(jnp.bfloat16), wq_ref[...]).astype(jnp.bfloat16)

    tm = qp.shape[0]
    slot = lax.broadcasted_iota(jnp.int32, (PEER_TOPK, tm), 0)
    K = PEER_TOPK
    n_mid = K // 2 - 1
    row = lax.broadcasted_iota(jnp.int32, (K + SUBLANES * n_mid + SUBLANES, tm), 0)
    mid_a = (row - K) // SUBLANES + 1
    tail = row >= K + SUBLANES * n_mid
    cand_a = jnp.where(row < K, 0, jnp.where(tail, K // 2 + row % SUBLANES, mid_a))
    cand_b = jnp.where(row < K, row, jnp.where(tail, 0, row % SUBLANES))
    in_stair = (cand_a + 1) * (cand_b + 1) <= K
    rowf = (cand_a * K + cand_b).astype(jnp.float32)

    def staircase(first_half, second_half):
        parts = [first_half[0:1] + second_half]
        parts += [first_half[a:a + 1] + second_half[0:SUBLANES] for a in range(1, n_mid + 1)]
        parts += [first_half[K // 2:K] + second_half[0:1]]
        return jnp.concatenate(parts, axis=0)

    for h in range(PEER_HEADS):
        blk = qp[:, h * PEER_QDIM:(h + 1) * PEER_QDIM]
        v1, i1 = _topk_rows(_dot_nt(keys_ref[2 * h], blk), PEER_TOPK)
        v2, i2 = _topk_rows(_dot_nt(keys_ref[2 * h + 1], blk), PEER_TOPK)
        cand = jnp.where(in_stair, staircase(v1, v2), -jnp.inf)
        eid = staircase(i1 * float(PEER_NKEYS), i2)
        top_s = jnp.zeros((PEER_TOPK, tm), jnp.float32)
        top_e = jnp.zeros((PEER_TOPK, tm), jnp.float32)
        for j in range(PEER_TOPK):
            m = jnp.max(cand, axis=0, keepdims=True)
            first = jnp.min(jnp.where(cand == m, rowf, 1e9), axis=0, keepdims=True)
            hit = rowf == first
            e = jnp.sum(jnp.where(hit, eid, 0.0), axis=0, keepdims=True)
            top_s = jnp.where(slot == j, m, top_s)
            top_e = jnp.where(slot == j, e, top_e)
            cand = jnp.where(hit, -jnp.inf, cand)
        ex = jnp.exp(top_s - jnp.max(top_s, axis=0, keepdims=True))
        gate_ref[h * PEER_TOPK:(h + 1) * PEER_TOPK, :] = ex / jnp.sum(ex, axis=0, keepdims=True)
        eid_ref[h * PEER_TOPK:(h + 1) * PEER_TOPK, :] = top_e.astype(jnp.int32)


def _out_proj_route(attn4, conv_n, x2, mod3, w_out_p, ln_g, wq, keys_x, seq):
    n = x2.shape[0]
    tm = TM_OUT
    tps = seq // tm
    row = lambda i: (i, 0)
    const2 = lambda i: (0, 0)
    return pl.pallas_call(
        _route_kernel,
        grid=(n // tm,),
        in_specs=[pl.BlockSpec((NSA_REP, tm, LANES), lambda i: (0, i, 0)),
                  pl.BlockSpec((tm, CONV_WIDTH), row),
                  pl.BlockSpec((tm, D_MODEL), row),
                  pl.BlockSpec((1, 6, D_MODEL), lambda i: (i // tps, 0, 0)),
                  pl.BlockSpec((D_MODEL, D_MODEL), const2),
                  pl.BlockSpec((1, D_MODEL), const2),
                  pl.BlockSpec((D_MODEL, PEER_HEADS * PEER_QDIM), const2),
                  pl.BlockSpec((2 * PEER_HEADS, PEER_NKEYS, PEER_QDIM), lambda i: (0, 0, 0))],
        out_specs=[pl.BlockSpec((tm, D_MODEL), row), pl.BlockSpec((tm, D_MODEL), row),
                   pl.BlockSpec((PEER_E, tm), lambda i: (0, i)),
                   pl.BlockSpec((PEER_E, tm), lambda i: (0, i))],
        out_shape=[jax.ShapeDtypeStruct((n, D_MODEL), jnp.float32),
                   jax.ShapeDtypeStruct((n, D_MODEL), jnp.float32),
                   jax.ShapeDtypeStruct((PEER_E, n), jnp.int32),
                   jax.ShapeDtypeStruct((PEER_E, n), jnp.float32)],
        compiler_params=_params(("arbitrary",)),
        name="out_proj_route",
    )(attn4, conv_n, x2, mod3, w_out_p, ln_g, wq, keys_x)


ROWS_PER_EXPERT = ROW_CHUNKS // 2


IDX_TILE = TT * PEER_E
ROW_BUFFERS = 3
TOKENS_PER_TRIP = 2 * ROW_BUFFERS


def _index_copy(idx_hbm, idx_smem, sem, tile, slot):
    dst = idx_smem.at[pl.ds(pl.multiple_of(slot * IDX_TILE, IDX_TILE), IDX_TILE)]
    return pltpu.make_async_copy(idx_hbm.at[tile], dst, sem.at[slot])


def _fetch_indices(idx_hbm, idx_smem, sem):
    i = pl.program_id(0)
    slot = i % 2

    @pl.when(i == 0)
    def _():
        _index_copy(idx_hbm, idx_smem, sem, 0, 0).start()

    _index_copy(idx_hbm, idx_smem, sem, i, slot).wait()

    @pl.when(i + 1 < pl.num_programs(0))
    def _():
        _index_copy(idx_hbm, idx_smem, sem, i + 1, 1 - slot).start()

    return slot


def _gather_rows(slot, t, idx_smem, tab_ref, rows_ref):
    tok_idx = idx_smem.at[pl.ds(pl.multiple_of(slot * IDX_TILE + t * PEER_E, PEER_E), PEER_E)]
    for k in range(PEER_E):
        e = tok_idx[k]
        row0 = pl.multiple_of(e, ROWS_PER_EXPERT)
        rows_ref[k * ROWS_PER_EXPERT:(k + 1) * ROWS_PER_EXPERT, :] = (
            tab_ref[pl.ds(row0, ROWS_PER_EXPERT), :])


def _token_stream(gather, compute, bufs):
    nb = len(bufs)
    gather(0, bufs[0])
    gather(1, bufs[1])

    def one(t, ring_pos):
        compute(t, bufs[ring_pos % nb])
        gather(jnp.minimum(t + 2, TT - 1), bufs[(ring_pos + 2) % nb])

    def trip(i, _):
        for j in range(TOKENS_PER_TRIP):
            one(TOKENS_PER_TRIP * i + j, j)
        return 0

    n_trips = TT // TOKENS_PER_TRIP
    lax.fori_loop(0, n_trips, trip, 0)
    for t in range(n_trips * TOKENS_PER_TRIP, TT):
        one(t, t)


def _split_bf16(x):
    hi = x.astype(jnp.bfloat16).astype(jnp.float32)
    return jnp.concatenate([hi, x - hi], axis=0).astype(jnp.bfloat16)


def _expert_of_row():
    n = lax.broadcasted_iota(jnp.int32, (PEER_E * ROW_CHUNKS, PEER_E), 0) // ROW_CHUNKS
    k = lax.broadcasted_iota(jnp.int32, (PEER_E * ROW_CHUNKS, PEER_E), 1)
    return jnp.where(n == k, 1.0, 0.0).astype(jnp.bfloat16)


def _peer_act_kernel(idx_hbm, x_ref, gate_ref, tab_ref, w_ref, idx_smem, sem, z_ref, *bufs):
    slot = _fetch_indices(idx_hbm, idx_smem, sem)
    m_i = lax.broadcasted_iota(jnp.int32, (2 * ROW_CHUNKS, PEER_E * ROW_CHUNKS), 0) % ROW_CHUNKS
    n_i = lax.broadcasted_iota(jnp.int32, (2 * ROW_CHUNKS, PEER_E * ROW_CHUNKS), 1) % ROW_CHUNKS
    chunk_match = m_i == n_i

    def compute(t, rows_ref):
        x16 = _split_bf16(x_ref[t])
        u = pltpu.bitcast(rows_ref[...], jnp.bfloat16)
        r = _dot_nt(x16, u)
        z_ref[pl.ds(t, 1), :] = jnp.sum(jnp.where(chunk_match, r, 0.0), axis=0, keepdims=True)

    gather = lambda t, rows_ref: _gather_rows(slot, t, idx_smem, tab_ref, rows_ref)
    _token_stream(gather, compute, bufs)
    z_hi, z_lo = _hi_lo(z_ref[...])
    owner = _expert_of_row()
    act = _dot(z_hi, owner) + _dot(z_lo, owner)
    w_ref[...] = gate_ref[...] * jax.nn.gelu(act)


def _peer_out_kernel(idx_hbm, w_ref, tab_ref, o_ref, idx_smem, sem, w8_ref, *bufs):
    slot = _fetch_indices(idx_hbm, idx_smem, sem)
    w_hi, w_lo = _hi_lo(w_ref[...])
    owner = _expert_of_row()
    w8_ref[...] = _dot_nt(w_hi, owner) + _dot_nt(w_lo, owner)
    m_i = lax.broadcasted_iota(jnp.int32, (ROW_CHUNKS, PEER_E * ROW_CHUNKS), 0)
    n_i = lax.broadcasted_iota(jnp.int32, (ROW_CHUNKS, PEER_E * ROW_CHUNKS), 1) % ROW_CHUNKS
    chunk_match = m_i == n_i

    def compute(t, rows_ref):
        lhs = _split_bf16(jnp.where(chunk_match, w8_ref[pl.ds(t, 1), :], 0.0))
        v = pltpu.bitcast(rows_ref[...], jnp.bfloat16)
        o = _dot(lhs, v)
        o_ref[t] = o[0:ROW_CHUNKS] + o[ROW_CHUNKS:2 * ROW_CHUNKS]

    gather = lambda t, rows_ref: _gather_rows(slot, t, idx_smem, tab_ref, rows_ref)
    _token_stream(gather, compute, bufs)


def _peer_scratch():
    rows = pltpu.VMEM((PEER_E * ROWS_PER_EXPERT, LANES), jnp.int32)
    return [pltpu.SMEM((2 * IDX_TILE,), jnp.int32), pltpu.SemaphoreType.DMA((2,)),
            pltpu.VMEM((TT, PEER_E * ROW_CHUNKS), jnp.float32)] + [rows] * ROW_BUFFERS


def _peer_act(idx_tiles, x8, gates, tab_u):
    n = x8.shape[0]
    return pl.pallas_call(
        _peer_act_kernel,
        grid=(n // TT,),
        in_specs=[pl.BlockSpec(memory_space=pl.ANY),
                  pl.BlockSpec((TT, ROW_CHUNKS, LANES), lambda i: (i, 0, 0)),
                  pl.BlockSpec((TT, PEER_E), lambda i: (i, 0)),
                  pl.BlockSpec(tab_u.shape, lambda i: (0, 0))],
        out_specs=pl.BlockSpec((TT, PEER_E), lambda i: (i, 0)),
        out_shape=jax.ShapeDtypeStruct((n, PEER_E), jnp.float32),
        scratch_shapes=_peer_scratch(),
        compiler_params=_params(("arbitrary",)),
        name="peer_activations",
    )(idx_tiles, x8, gates, tab_u)


def _peer_out(idx_tiles, w, tab_v):
    n = w.shape[0]
    return pl.pallas_call(
        _peer_out_kernel,
        grid=(n // TT,),
        in_specs=[pl.BlockSpec(memory_space=pl.ANY),
                  pl.BlockSpec((TT, PEER_E), lambda i: (i, 0)),
                  pl.BlockSpec(tab_v.shape, lambda i: (0, 0))],
        out_specs=pl.BlockSpec((TT, ROW_CHUNKS, LANES), lambda i: (i, 0, 0)),
        out_shape=jax.ShapeDtypeStruct((n, ROW_CHUNKS, LANES), jnp.float32),
        scratch_shapes=_peer_scratch(),
        compiler_params=_params(("arbitrary",)),
        name="peer_combine",
    )(idx_tiles, w, tab_v)


def _pack_table(tab):
    bits = lax.bitcast_convert_type(tab.astype(jnp.bfloat16), jnp.uint16).astype(jnp.uint32)
    bits = bits.reshape(tab.shape[0], ROWS_PER_EXPERT, 2, LANES)
    words = bits[:, :, 0, :] | (bits[:, :, 1, :] << 16)
    return lax.bitcast_convert_type(words, jnp.int32).reshape(tab.shape[0] * ROWS_PER_EXPERT, LANES)


def _final_kernel(x1_ref, p_ref, mod_ref, g_ref, o_ref):
    x2 = x1_ref[...] + mod_ref[0, 5:6, :] * p_ref[...]
    o_ref[...] = _rms(x2, D_MODEL) * g_ref[...]


def _final(x1, peer, mod3, g, seq):
    n = x1.shape[0]
    tm = TM_PROJ
    tps = seq // tm
    row = lambda i: (i, 0)
    return pl.pallas_call(
        _final_kernel,
        grid=(n // tm,),
        in_specs=[pl.BlockSpec((tm, D_MODEL), row), pl.BlockSpec((tm, D_MODEL), row),
                  pl.BlockSpec((1, 6, D_MODEL), lambda i: (i // tps, 0, 0)),
                  pl.BlockSpec((1, D_MODEL), lambda i: (0, 0))],
        out_specs=pl.BlockSpec((tm, D_MODEL), row),
        out_shape=jax.ShapeDtypeStruct((n, D_MODEL), jnp.float32),
        compiler_params=_params(("arbitrary",)),
        name="final_norm",
    )(x1, peer, mod3, g)


def _attn_perm():
    idx = []
    for r in range(NSA_REP):
        for g in range(NSA_KV_HEADS):
            h = g * NSA_REP + r
            idx.extend(range(h * HEAD_DIM, (h + 1) * HEAD_DIM))
    return np.asarray(idx, np.int32)


def _rel_bucket_np(dist):
    n = np.maximum(dist, 0)
    exact = REL_BUCKETS // 2
    log_ratio = (np.log(np.maximum(n, 1).astype(np.float32) / np.float32(exact))
                 / np.float32(math.log(REL_MAX_DIST / exact)))
    large = exact + (log_ratio * np.float32(REL_BUCKETS - exact)).astype(np.int32)
    return np.where(n < exact, n, np.minimum(large, REL_BUCKETS - 1)).astype(np.int32)


def _bias_tables(rel_table, seq):
    nq = seq // TQ
    ncp = seq // CMP_STRIDE

    def lookup(bucket):
        onehot = (jnp.asarray(bucket)[..., None] == jnp.arange(REL_BUCKETS)).astype(jnp.float32)
        out = jnp.einsum("...b,bh->...h", onehot, rel_table, precision=HIGHEST)
        return out.reshape(bucket.shape + (NSA_KV_HEADS, NSA_REP))
    n_near = -(-REL_MAX_DIST // TQ) + 1
    r_i = np.arange(TQ)[:, None]
    c_i = np.arange(TK)[None, :]
    near = np.stack([_rel_bucket_np(d * TQ + r_i - c_i) for d in range(n_near)]
                    + [np.full((TQ, TK), REL_BUCKETS - 1, np.int32)])
    b_near = jnp.transpose(lookup(near), (3, 0, 4, 1, 2))
    b_near = b_near.reshape(NSA_KV_HEADS, n_near + 1, NSA_REP * TQ, TK)
    t_i = np.arange(seq)[:, None]
    n_i = np.arange(ncp)[None, :]
    cmp_b = _rel_bucket_np(t_i - (n_i * CMP_STRIDE + CMP_BLK - 1))
    b_cmp = jnp.transpose(lookup(cmp_b), (2, 3, 0, 1))
    b_cmp = b_cmp.reshape(NSA_KV_HEADS, NSA_REP, nq, TQ, ncp).transpose(0, 2, 1, 3, 4)
    b_cmp = b_cmp.reshape(NSA_KV_HEADS, nq, NSA_REP * TQ, ncp)
    return b_cmp, b_near


def _cmp_to_slc(seq):
    ncp = seq // CMP_STRIDE
    n_cmp = (seq - CMP_BLK) // CMP_STRIDE + 1
    n_slc = seq // SEL_BLK
    cs = np.arange(ncp) * CMP_STRIDE
    ss = np.arange(LANES) * SEL_BLK
    ov = (cs[:, None] < ss[None, :] + SEL_BLK) & (cs[:, None] + CMP_BLK > ss[None, :])
    ov &= (np.arange(ncp)[:, None] < n_cmp) & (np.arange(LANES)[None, :] < n_slc)
    return jnp.asarray(ov.T, jnp.bfloat16)


def kernel(x, c, ln_mix_g, ln_ffn_g, w_mod, b_mod, w_in, cmp_pe_k, cmp_pe_v, cmp_wk1, cmp_wk2,
           cmp_wv1, cmp_wv2, conv_w, norm_attn_g, norm_conv_g, w_out, peer_wq, peer_keys, peer_u,
           peer_v, rel_table, ln_final_g):
    B, S, _ = x.shape
    n = B * S
    assert w_mod.shape[0] == 1 and S % TM_PROJ == 0 and S % TQ == 0 and n % TT == 0
    assert TQ == TK and WINDOW % TK == 0
    x2 = x.reshape(n, D_MODEL)
    perm = _attn_perm()

    mod3 = _modulation(c, w_mod[0], b_mod[0]).reshape(B, 6, D_MODEL)

    w = w_in[0]
    q_cols = w[:, :NSA_WIDTH][:, perm]
    kv_cols = w[:, NSA_WIDTH:NSA_WIDTH + 6 * KV_W]
    g0 = NSA_WIDTH + 6 * KV_W
    gate_cols = jnp.pad(w[:, g0:g0 + NSA_HEADS * 3], ((0, 0), (0, LANES - NSA_HEADS * 3)))
    conv_cols = w[:, g0 + NSA_HEADS * 3:]
    w_perm = jnp.concatenate([q_cols, kv_cols, gate_cols, conv_cols], axis=1).astype(jnp.bfloat16)

    q4, kc, vc, ks, vs, kw, vw, gates, conv_n = _in_projection(
        x2, mod3, ln_mix_g[0].reshape(1, D_MODEL), w_perm, conv_w[0].reshape(CONV_K, CONV_WIDTH),
        norm_conv_g[0].reshape(1, CONV_WIDTH), S)

    ncp = S // CMP_STRIDE
    tok = jnp.stack([kc, vc]).reshape(2, B, ncp, CMP_STRIDE, NSA_KV_HEADS, HEAD_DIM)
    tok = tok.transpose(0, 1, 4, 2, 3, 5).reshape(2, B, NSA_KV_HEADS, ncp, CMP_STRIDE * HEAD_DIM)
    cmp_kv = _compress(tok,
                       jnp.stack([cmp_pe_k[0], cmp_pe_v[0]]).reshape(2, 1, CMP_BLK * HEAD_DIM),
                       jnp.stack([cmp_wk1[0], cmp_wv1[0]]), jnp.stack([cmp_wk2[0], cmp_wv2[0]]))

    b_cmp, b_near = _bias_tables(rel_table, S)
    attn4 = _attention(q4, ks, vs, kw, vw, cmp_kv, gates, b_cmp, b_near, _cmp_to_slc(S),
                       norm_attn_g[0][perm].reshape(NSA_REP, 1, LANES), B, S)

    w_out_p = jnp.concatenate([w_out[0][:NSA_WIDTH][perm], w_out[0][NSA_WIDTH:]], axis=0)
    half = PEER_QDIM // 2
    keys = peer_keys[0].reshape(2 * PEER_HEADS, PEER_NKEYS, half)
    lo = jnp.pad(keys, ((0, 0), (0, 0), (0, half)))
    hi = jnp.pad(keys, ((0, 0), (0, 0), (half, 0)))
    is_hi = (jnp.arange(2 * PEER_HEADS) % 2 == 1)[:, None, None]
    keys_x = jnp.where(is_hi, hi, lo).astype(jnp.bfloat16)
    x1, h2, eid_t, gate_t = _out_proj_route(
        attn4, conv_n, x2, mod3, w_out_p.astype(jnp.bfloat16), ln_ffn_g[0].reshape(1, D_MODEL),
        peer_wq[0].astype(jnp.bfloat16), keys_x, S)

    idx_tiles = (eid_t.T * ROWS_PER_EXPERT).reshape(n // TT, TT * PEER_E)
    w_act = _peer_act(idx_tiles, h2.reshape(n, ROW_CHUNKS, LANES), gate_t.T, _pack_table(peer_u[0]))
    peer = _peer_out(idx_tiles, w_act, _pack_table(peer_v[0])).reshape(n, D_MODEL)

    out = _final(x1, peer, mod3, ln_final_g.reshape(1, D_MODEL), S)
    return out.reshape(B, S, D_MODEL)
```

```python
import functools
import math

import numpy as np
import jax
import jax.numpy as jnp
from jax import lax
from jax.experimental import pallas as pl
from jax.experimental.pallas import tpu as pltpu

D_MODEL = 1024
HEAD_DIM = 64
NSA_HEADS = 8
NSA_KV_HEADS = 2
NSA_REP = NSA_HEADS // NSA_KV_HEADS
NSA_WIDTH = NSA_HEADS * HEAD_DIM
KV_W = NSA_KV_HEADS * HEAD_DIM
CONV_WIDTH = D_MODEL - NSA_WIDTH
CONV_K = 3
CMP_BLK = 32
CMP_STRIDE = 16
SEL_BLK = 64
SEL_TOPN = 8
WINDOW = 512
FORCE_SCORE = 1e4
REL_BUCKETS = 32
REL_MAX_DIST = 128
PEER_HEADS = 8
PEER_NKEYS = 128
PEER_EXPERTS = PEER_NKEYS * PEER_NKEYS
PEER_QDIM = 128
PEER_TOPK = 16
PEER_E = PEER_HEADS * PEER_TOPK
EPS = 1e-6

LANES = 128
SUBLANES = 8
ROW_CHUNKS = D_MODEL // LANES
VMEM_LIMIT = 56 * 1024 * 1024

TM_PROJ = 512
TM_OUT = 256
TQ = 256
TK = 256
TT = 128
NEG = -1e30
HIGHEST = lax.Precision.HIGHEST

C_Q = 0
C_KV = C_Q + NSA_WIDTH
C_GATE = C_KV + 6 * KV_W
C_CONV = C_GATE + LANES
N_COLS = C_CONV + 3 * CONV_WIDTH


def _params(sem):
    return pltpu.CompilerParams(dimension_semantics=sem, vmem_limit_bytes=VMEM_LIMIT)


def _dot(a, b, precision=None):
    return jnp.dot(a, b, preferred_element_type=jnp.float32, precision=precision)


def _dot_nt(a, b, precision=None):
    return lax.dot_general(a, b, (((1,), (1,)), ((), ())),
                           preferred_element_type=jnp.float32, precision=precision)


def _rms(x, n):
    return x * lax.rsqrt(jnp.sum(x * x, axis=-1, keepdims=True) * (1.0 / n) + EPS)


def _mod_kernel(c_ref, w_ref, b_ref, o_ref):
    c = c_ref[...]
    act = c * jax.nn.sigmoid(c)
    o_ref[...] = _dot(act, w_ref[...], HIGHEST) + b_ref[...]


def _modulation(c, w_mod, b_mod):
    B = c.shape[0]
    n = w_mod.shape[1]
    bn = D_MODEL
    return pl.pallas_call(
        _mod_kernel,
        grid=(n // bn,),
        in_specs=[pl.BlockSpec((B, D_MODEL), lambda j: (0, 0)),
                  pl.BlockSpec((D_MODEL, bn), lambda j: (0, j)),
                  pl.BlockSpec((1, bn), lambda j: (0, j))],
        out_specs=pl.BlockSpec((B, bn), lambda j: (0, j)),
        out_shape=jax.ShapeDtypeStruct((B, n), jnp.float32),
        compiler_params=_params(("arbitrary",)),
        name="modulation",
    )(c, w_mod, b_mod.reshape(1, n))


def _inproj_kernel(tiles_per_seq, x_ref, mod_ref, g_ref, w_ref, cw_ref, cg_ref,
                   q_ref, kc_ref, vc_ref, ks_ref, vs_ref, kw_ref, vw_ref, gate_ref, conv_ref,
                   carry_ref):
    i = pl.program_id(0)
    x = x_ref[...]
    sh = mod_ref[0, 0:1, :]
    sc = mod_ref[0, 1:2, :]
    h = (_rms(x, D_MODEL) * g_ref[...]) * (1.0 + sc) + sh
    proj = _dot(h.astype(jnp.bfloat16), w_ref[...])

    for r in range(NSA_REP):
        q_ref[r] = (proj[:, C_Q + r * LANES:C_Q + (r + 1) * LANES] * (HEAD_DIM ** -0.5)
                    ).astype(jnp.bfloat16)
    kv = lambda k: proj[:, C_KV + k * KV_W:C_KV + (k + 1) * KV_W]
    kc_ref[...] = kv(0)
    vc_ref[...] = kv(1)
    ks_ref[...] = kv(2).astype(jnp.bfloat16)
    vs_ref[...] = kv(3).astype(jnp.bfloat16)
    kw_ref[...] = kv(4).astype(jnp.bfloat16)
    vw_ref[...] = kv(5).astype(jnp.bfloat16)
    gate_ref[...] = jax.nn.sigmoid(proj[:, C_GATE:C_GATE + LANES])

    cb = proj[:, C_CONV:C_CONV + CONV_WIDTH]
    cc = proj[:, C_CONV + CONV_WIDTH:C_CONV + 2 * CONV_WIDTH]
    ch = proj[:, C_CONV + 2 * CONV_WIDTH:C_CONV + 3 * CONV_WIDTH]
    z = cc * ch
    tm = z.shape[0]

    @pl.when(i % tiles_per_seq == 0)
    def _():
        carry_ref[...] = jnp.zeros_like(carry_ref)

    prev1 = carry_ref[SUBLANES - 1:SUBLANES, :]
    prev2 = carry_ref[SUBLANES - 2:SUBLANES - 1, :]
    row = lax.broadcasted_iota(jnp.int32, (tm, CONV_WIDTH), 0)
    z1 = jnp.where(row == 0, prev1, pltpu.roll(z, 1, 0))
    z2 = pltpu.roll(z, 2, 0)
    z2 = jnp.where(row == 0, prev2, jnp.where(row == 1, prev1, z2))
    carry_ref[...] = z[tm - SUBLANES:, :]
    y = cw_ref[0:1, :] * z2 + cw_ref[1:2, :] * z1 + cw_ref[2:3, :] * z
    conv_ref[...] = (_rms(cb * y, CONV_WIDTH) * cg_ref[...]).astype(jnp.bfloat16)


def _in_projection(x2, mod3, ln_g, w_perm, conv_w, conv_g, seq):
    n = x2.shape[0]
    tm = TM_PROJ
    tps = seq // tm
    row = lambda i: (i, 0)
    kv_f32 = jax.ShapeDtypeStruct((n, KV_W), jnp.float32)
    kv_bf = jax.ShapeDtypeStruct((n, KV_W), jnp.bfloat16)
    return pl.pallas_call(
        functools.partial(_inproj_kernel, tps),
        grid=(n // tm,),
        in_specs=[pl.BlockSpec((tm, D_MODEL), row),
                  pl.BlockSpec((1, 6, D_MODEL), lambda i: (i // tps, 0, 0)),
                  pl.BlockSpec((1, D_MODEL), lambda i: (0, 0)),
                  pl.BlockSpec((D_MODEL, N_COLS), lambda i: (0, 0)),
                  pl.BlockSpec((CONV_K, CONV_WIDTH), lambda i: (0, 0)),
                  pl.BlockSpec((1, CONV_WIDTH), lambda i: (0, 0))],
        out_specs=[pl.BlockSpec((NSA_REP, tm, LANES), lambda i: (0, i, 0))]
                  + [pl.BlockSpec((tm, KV_W), row)] * 6
                  + [pl.BlockSpec((tm, LANES), row), pl.BlockSpec((tm, CONV_WIDTH), row)],
        out_shape=[jax.ShapeDtypeStruct((NSA_REP, n, LANES), jnp.bfloat16),
                   kv_f32, kv_f32, kv_bf, kv_bf, kv_bf, kv_bf,
                   jax.ShapeDtypeStruct((n, LANES), jnp.float32),
                   jax.ShapeDtypeStruct((n, CONV_WIDTH), jnp.bfloat16)],
        scratch_shapes=[pltpu.VMEM((SUBLANES, CONV_WIDTH), jnp.float32)],
        compiler_params=_params(("arbitrary",)),
        name="in_projection",
    )(x2, mod3, ln_g, w_perm, conv_w, conv_g)


def _compress_kernel(tok_ref, pe_ref, w1_ref, w2_ref, o_ref):
    half = (CMP_BLK // 2) * HEAD_DIM
    w1a = w1_ref[0, 0:half, :]
    w1b = w1_ref[0, half:2 * half, :]
    pe_term = _dot(pe_ref[0], w1_ref[0], HIGHEST)
    ncp = tok_ref.shape[3]
    row = lax.broadcasted_iota(jnp.int32, (ncp, HEAD_DIM), 0)
    for g in range(NSA_KV_HEADS):
        t = tok_ref[0, 0, g]
        a = _dot(t, w1a, HIGHEST)
        b = _dot(t, w1b, HIGHEST)
        pre = a + pltpu.roll(b, ncp - 1, 0) + pe_term
        out = _dot(jax.nn.gelu(pre), w2_ref[0], HIGHEST)
        o_ref[0, 0, :, g * HEAD_DIM:(g + 1) * HEAD_DIM] = jnp.where(row < ncp - 1, out, 0.0)


def _compress(tok, pe, w1, w2):
    _, B, G, ncp, width = tok.shape
    return pl.pallas_call(
        _compress_kernel,
        grid=(2, B),
        in_specs=[pl.BlockSpec((1, 1, G, ncp, width), lambda k, b: (k, b, 0, 0, 0)),
                  pl.BlockSpec((1, 1, CMP_BLK * HEAD_DIM), lambda k, b: (k, 0, 0)),
                  pl.BlockSpec((1, CMP_BLK * HEAD_DIM, HEAD_DIM), lambda k, b: (k, 0, 0)),
                  pl.BlockSpec((1, HEAD_DIM, HEAD_DIM), lambda k, b: (k, 0, 0))],
        out_specs=pl.BlockSpec((1, 1, ncp, KV_W), lambda k, b: (k, b, 0, 0)),
        out_shape=jax.ShapeDtypeStruct((2, B, ncp, KV_W), jnp.float32),
        compiler_params=_params(("arbitrary", "arbitrary")),
        name="compress_kv",
    )(tok, pe, w1, w2)


M_FLOOR = -1e29


def _hi_lo(x):
    hi = x.astype(jnp.bfloat16)
    return hi, (x - hi.astype(jnp.float32)).astype(jnp.bfloat16)


def _attn_kernel(n_slc, q_ref, ks_ref, vs_ref, kw_ref, vw_ref, cmp_ref, gate_ref,
                 bcmp_ref, bnear_ref, c2s_ref, ng_ref, o_ref, og0_ref):
    qi = pl.program_id(1)
    t0 = qi * TQ
    rows = NSA_REP * TQ
    ncp = cmp_ref.shape[2]
    q4 = q_ref[...].reshape(rows, LANES)
    lane = lax.broadcasted_iota(jnp.int32, (1, LANES), 1)
    tq_col = t0 + lax.broadcasted_iota(jnp.int32, (TQ, 1), 0)
    t_col = jnp.concatenate([tq_col] * NSA_REP, axis=0)
    rowq = lax.broadcasted_iota(jnp.int32, (TQ, 1), 0)
    colk = lax.broadcasted_iota(jnp.int32, (1, TK), 1)
    gates_hi, gates_lo = _hi_lo(gate_ref[...])
    n_near = bnear_ref.shape[1] - 1

    groups = []
    for g in range(NSA_KV_HEADS):
        in_g = (lane // HEAD_DIM) == g
        q = jnp.where(in_g, q4, jnp.zeros_like(q4))

        kc_hi, kc_lo = _hi_lo(cmp_ref[0, 0])
        s = _dot_nt(q, kc_hi) + _dot_nt(q, kc_lo) + bcmp_ref[g, 0]
        n_idx = lax.broadcasted_iota(jnp.int32, (1, ncp), 1)
        valid = (n_idx * CMP_STRIDE + (CMP_BLK - 1) <= t_col) & (n_idx < ncp - 1)
        s = jnp.where(valid, s, NEG)
        e = jnp.where(valid, jnp.exp(s - jnp.max(s, axis=-1, keepdims=True)), 0.0)
        e_hi, e_lo = _hi_lo(e)
        ones = jnp.ones((ncp, ncp), jnp.bfloat16)
        den = _dot(e_hi, ones) + _dot(e_lo, ones)
        p_cmp = e / jnp.maximum(den, 1e-30)
        o_cmp = _dot(p_cmp.astype(jnp.bfloat16), cmp_ref[1, 0].astype(jnp.bfloat16))

        p_sum = p_cmp[0:TQ]
        for r in range(1, NSA_REP):
            p_sum = p_sum + p_cmp[r * TQ:(r + 1) * TQ]
        ps_hi, ps_lo = _hi_lo(p_sum)
        imp = (_dot_nt(c2s_ref[...], ps_hi) + _dot_nt(c2s_ref[...], ps_lo))[0:n_slc]
        blk = lax.broadcasted_iota(jnp.int32, (n_slc, TQ), 0)
        blk_f = blk.astype(jnp.float32)
        tok = t0 + lax.broadcasted_iota(jnp.int32, (1, TQ), 1)
        cur = tok // SEL_BLK
        forced = (blk == 0) | (blk == cur) | (blk == cur - 1)
        score = jnp.where(forced, FORCE_SCORE, jnp.where(blk * SEL_BLK <= tok, imp, -jnp.inf))
        taken = jnp.zeros((n_slc, TQ), jnp.bool_)
        for _ in range(min(SEL_TOPN, n_slc)):
            eff = jnp.where(taken, -jnp.inf, score)
            mx = jnp.max(eff, axis=0, keepdims=True)
            cand = jnp.where((eff == mx) & jnp.logical_not(taken), blk_f, float(LANES))
            taken = taken | (blk_f == jnp.min(cand, axis=0, keepdims=True))
        sel_t = jnp.concatenate([jnp.where(taken, 1.0, 0.0),
                                 jnp.zeros((LANES - n_slc, TQ), jnp.float32)], axis=0)
        sel = sel_t.T.astype(jnp.bfloat16)
        groups.append((in_g, q, o_cmp, sel))

    blk_row = lax.broadcasted_iota(jnp.int32, (LANES, TK), 0)
    blk_col = lax.broadcasted_iota(jnp.int32, (LANES, TK), 1) // SEL_BLK
    causal_add = jnp.where(rowq - colk >= 0, 0.0, NEG)
    edge_add = jnp.where(rowq - colk < 0, 0.0, NEG)

    def step(g, k, v, kt, mask_add, carry):
        in_g, q = groups[g][0], groups[g][1]
        m_old, acc_old = carry
        s = _dot_nt(q, k) + bnear_ref[g, jnp.minimum(qi - kt, n_near)]
        s = (s.reshape(NSA_REP, TQ, TK) + mask_add[None]).reshape(rows, TK)
        m_new = jnp.maximum(m_old, jnp.max(s, axis=-1, keepdims=True))
        p = jnp.exp(s - m_new).astype(jnp.bfloat16)
        v_aug = jnp.where(in_g, v, jnp.ones((TK, LANES), jnp.bfloat16))
        return m_new, jnp.exp(m_old - m_new) * acc_old + _dot(p, v_aug)

    def slc_step(kt, carry):
        at = pl.ds(pl.multiple_of(kt * TK, TK), TK)
        k, v = ks_ref[at, :], vs_ref[at, :]
        expand = jnp.where(blk_row == blk_col + kt * (TK // SEL_BLK), 1.0, 0.0).astype(jnp.bfloat16)
        diag = jnp.where(kt == qi, causal_add, 0.0)
        out = []
        for g in range(NSA_KV_HEADS):
            chosen = _dot(groups[g][3], expand)
            out.append(step(g, k, v, kt, (chosen - 1.0) * (-NEG) + diag, carry[g]))
        return tuple(out)

    def win_step(kt, carry):
        at = pl.ds(pl.multiple_of(kt * TK, TK), TK)
        k, v = kw_ref[at, :], vw_ref[at, :]
        mask_add = jnp.where(kt == qi, causal_add,
                             jnp.where(kt == qi - WINDOW // TK, edge_add, 0.0))
        return tuple(step(g, k, v, kt, mask_add, carry[g]) for g in range(NSA_KV_HEADS))

    def finish(acc):
        return acc / jnp.maximum(pltpu.roll(acc, HEAD_DIM, 1), 1e-30)

    init = ((jnp.full((rows, 1), M_FLOOR, jnp.float32), jnp.zeros((rows, LANES), jnp.float32)),
            ) * NSA_KV_HEADS
    slc = lax.fori_loop(0, qi + 1, slc_step, init)
    win = lax.fori_loop(jnp.maximum(qi - WINDOW // TK, 0), qi + 1, win_step, init)

    for g in range(NSA_KV_HEADS):
        in_g, _, o_cmp, _ = groups[g]
        o_slc = finish(slc[g][1])
        o_win = finish(win[g][1])

        n_gate = NSA_REP * 3
        src = lax.broadcasted_iota(jnp.int32, (LANES, n_gate * LANES), 0)
        dst = lax.broadcasted_iota(jnp.int32, (LANES, n_gate * LANES), 1) // LANES
        spread = jnp.where(src == g * n_gate + dst, 1.0, 0.0).astype(jnp.bfloat16)
        gate_x = _dot(gates_hi, spread) + _dot(gates_lo, spread)
        for r in range(NSA_REP):
            sl = slice(r * TQ, (r + 1) * TQ)
            gate = lambda j: gate_x[:, (r * 3 + j) * LANES:(r * 3 + j + 1) * LANES]
            o = gate(0) * o_cmp[sl] + gate(1) * o_slc[sl] + gate(2) * o_win[sl]
            if g == 0:
                og0_ref[r] = o
            else:
                og0_ref[r] = jnp.where(in_g, o, og0_ref[r])

    ss = jnp.zeros((TQ, 1), jnp.float32)
    for r in range(NSA_REP):
        o = og0_ref[r]
        ss = ss + jnp.sum(o * o, axis=-1, keepdims=True)
    inv = lax.rsqrt(ss * (1.0 / NSA_WIDTH) + EPS)
    for r in range(NSA_REP):
        o_ref[r] = (og0_ref[r] * inv * ng_ref[r]).astype(jnp.bfloat16)


def _attention(q4, ks, vs, kw, vw, cmp_kv, gates, bias_cmp, bias_near, c2s, norm_g, batch, seq):
    n = batch * seq
    nq = seq // TQ
    ncp = cmp_kv.shape[2]
    n_slc = seq // SEL_BLK
    kv_spec = pl.BlockSpec((seq, KV_W), lambda b, i: (b, 0))
    return pl.pallas_call(
        functools.partial(_attn_kernel, n_slc),
        grid=(batch, nq),
        in_specs=[pl.BlockSpec((NSA_REP, TQ, LANES), lambda b, i: (0, b * nq + i, 0)),
                  kv_spec, kv_spec, kv_spec, kv_spec,
                  pl.BlockSpec((2, 1, ncp, KV_W), lambda b, i: (0, b, 0, 0)),
                  pl.BlockSpec((TQ, LANES), lambda b, i: (b * nq + i, 0)),
                  pl.BlockSpec((NSA_KV_HEADS, 1, NSA_REP * TQ, ncp), lambda b, i: (0, i, 0, 0)),
                  pl.BlockSpec(bias_near.shape, lambda b, i: (0, 0, 0, 0)),
                  pl.BlockSpec(c2s.shape, lambda b, i: (0, 0)),
                  pl.BlockSpec((NSA_REP, 1, LANES), lambda b, i: (0, 0, 0))],
        out_specs=pl.BlockSpec((NSA_REP, TQ, LANES), lambda b, i: (0, b * nq + i, 0)),
        out_shape=jax.ShapeDtypeStruct((NSA_REP, n, LANES), jnp.bfloat16),
        scratch_shapes=[pltpu.VMEM((NSA_REP, TQ, LANES), jnp.float32)],
        compiler_params=_params(("arbitrary", "arbitrary")),
        name="sparse_attention",
    )(q4, ks, vs, kw, vw, cmp_kv, gates, bias_cmp, bias_near, c2s, norm_g)


def _topk_rows(s, k):
    nrow = s.shape[0]
    rowf = lax.broadcasted_iota(jnp.int32, s.shape, 0).astype(jnp.float32)
    slot = lax.broadcasted_iota(jnp.int32, (k, s.shape[1]), 0)
    vals = jnp.zeros((k, s.shape[1]), jnp.float32)
    idxs = jnp.zeros((k, s.shape[1]), jnp.float32)
    for j in range(k):
        m = jnp.max(s, axis=0, keepdims=True)
        first = jnp.min(jnp.where(s == m, rowf, float(nrow)), axis=0, keepdims=True)
        vals = jnp.where(slot == j, m, vals)
        idxs = jnp.where(slot == j, first, idxs)
        s = jnp.where(rowf == first, -jnp.inf, s)
    return vals, idxs


def _route_kernel(attn_ref, conv_ref, x_ref, mod_ref, wout_ref, g_ref, wq_ref, keys_ref,
                  x1_ref, h2_ref, eid_ref, gate_ref):
    mixed = jnp.concatenate([attn_ref[r] for r in range(NSA_REP)] + [conv_ref[...]], axis=-1)
    x1 = x_ref[...] + mod_ref[0, 2:3, :] * _dot(mixed, wout_ref[...])
    x1_ref[...] = x1
    h2 = (_rms(x1, D_MODEL) * g_ref[...]) * (1.0 + mod_ref[0, 4:5, :]) + mod_ref[0, 3:4, :]
    h2_ref[...] = h2
    qp = _dot(h2.astype(jnp.bfloat16), wq_ref[...]).astype(jnp.bfloat16)

    tm = qp.shape[0]
    slot = lax.broadcasted_iota(jnp.int32, (PEER_TOPK, tm), 0)
    K = PEER_TOPK
    n_mid = K // 2 - 1
    row = lax.broadcasted_iota(jnp.int32, (K + SUBLANES * n_mid + SUBLANES, tm), 0)
    mid_a = (row - K) // SUBLANES + 1
    tail = row >= K + SUBLANES * n_mid
    cand_a = jnp.where(row < K, 0, jnp.where(tail, K // 2 + row % SUBLANES, mid_a))
    cand_b = jnp.where(row < K, row, jnp.where(tail, 0, row % SUBLANES))
    in_stair = (cand_a + 1) * (cand_b + 1) <= K
    rowf = (cand_a * K + cand_b).astype(jnp.float32)

    def staircase(first_half, second_half):
        parts = [first_half[0:1] + second_half]
        parts += [first_half[a:a + 1] + second_half[0:SUBLANES] for a in range(1, n_mid + 1)]
        parts += [first_half[K // 2:K] + second_half[0:1]]
        return jnp.concatenate(parts, axis=0)

    for h in range(PEER_HEADS):
        blk = qp[:, h * PEER_QDIM:(h + 1) * PEER_QDIM]
        v1, i1 = _topk_rows(_dot_nt(keys_ref[2 * h], blk), PEER_TOPK)
        v2, i2 = _topk_rows(_dot_nt(keys_ref[2 * h + 1], blk), PEER_TOPK)
        cand = jnp.where(in_stair, staircase(v1, v2), -jnp.inf)
        eid = staircase(i1 * float(PEER_NKEYS), i2)
        top_s = jnp.zeros((PEER_TOPK, tm), jnp.float32)
        top_e = jnp.zeros((PEER_TOPK, tm), jnp.float32)
        for j in range(PEER_TOPK):
            m = jnp.max(cand, axis=0, keepdims=True)
            first = jnp.min(jnp.where(cand == m, rowf, 1e9), axis=0, keepdims=True)
            hit = rowf == first
            e = jnp.sum(jnp.where(hit, eid, 0.0), axis=0, keepdims=True)
            top_s = jnp.where(slot == j, m, top_s)
            top_e = jnp.where(slot == j, e, top_e)
            cand = jnp.where(hit, -jnp.inf, cand)
        ex = jnp.exp(top_s - jnp.max(top_s, axis=0, keepdims=True))
        gate_ref[h * PEER_TOPK:(h + 1) * PEER_TOPK, :] = ex / jnp.sum(ex, axis=0, keepdims=True)
        eid_ref[h * PEER_TOPK:(h + 1) * PEER_TOPK, :] = top_e.astype(jnp.int32)


def _out_proj_route(attn4, conv_n, x2, mod3, w_out_p, ln_g, wq, keys_x, seq):
    n = x2.shape[0]
    tm = TM_OUT
    tps = seq // tm
    row = lambda i: (i, 0)
    const2 = lambda i: (0, 0)
    return pl.pallas_call(
        _route_kernel,
        grid=(n // tm,),
        in_specs=[pl.BlockSpec((NSA_REP, tm, LANES), lambda i: (0, i, 0)),
                  pl.BlockSpec((tm, CONV_WIDTH), row),
                  pl.BlockSpec((tm, D_MODEL), row),
                  pl.BlockSpec((1, 6, D_MODEL), lambda i: (i // tps, 0, 0)),
                  pl.BlockSpec((D_MODEL, D_MODEL), const2),
                  pl.BlockSpec((1, D_MODEL), const2),
                  pl.BlockSpec((D_MODEL, PEER_HEADS * PEER_QDIM), const2),
                  pl.BlockSpec((2 * PEER_HEADS, PEER_NKEYS, PEER_QDIM), lambda i: (0, 0, 0))],
        out_specs=[pl.BlockSpec((tm, D_MODEL), row), pl.BlockSpec((tm, D_MODEL), row),
                   pl.BlockSpec((PEER_E, tm), lambda i: (0, i)),
                   pl.BlockSpec((PEER_E, tm), lambda i: (0, i))],
        out_shape=[jax.ShapeDtypeStruct((n, D_MODEL), jnp.float32),
                   jax.ShapeDtypeStruct((n, D_MODEL), jnp.float32),
                   jax.ShapeDtypeStruct((PEER_E, n), jnp.int32),
                   jax.ShapeDtypeStruct((PEER_E, n), jnp.float32)],
        compiler_params=_params(("arbitrary",)),
        name="out_proj_route",
    )(attn4, conv_n, x2, mod3, w_out_p, ln_g, wq, keys_x)


ROWS_PER_EXPERT = ROW_CHUNKS // 2


IDX_TILE = TT * PEER_E
ROW_BUFFERS = 3
TOKENS_PER_TRIP = 2 * ROW_BUFFERS


def _index_copy(idx_hbm, idx_smem, sem, tile, slot):
    dst = idx_smem.at[pl.ds(pl.multiple_of(slot * IDX_TILE, IDX_TILE), IDX_TILE)]
    return pltpu.make_async_copy(idx_hbm.at[tile], dst, sem.at[slot])


def _fetch_indices(idx_hbm, idx_smem, sem):
    i = pl.program_id(0)
    slot = i % 2

    @pl.when(i == 0)
    def _():
        _index_copy(idx_hbm, idx_smem, sem, 0, 0).start()

    _index_copy(idx_hbm, idx_smem, sem, i, slot).wait()

    @pl.when(i + 1 < pl.num_programs(0))
    def _():
        _index_copy(idx_hbm, idx_smem, sem, i + 1, 1 - slot).start()

    return slot


def _gather_rows(slot, t, idx_smem, tab_ref, rows_ref):
    base = slot * IDX_TILE + t * PEER_E
    for k in range(PEER_E):
        if k % SUBLANES == 0:
            view = idx_smem.at[pl.ds(pl.multiple_of(base + k, SUBLANES), SUBLANES)]
        e = view[k % SUBLANES]
        row0 = pl.multiple_of(e, ROWS_PER_EXPERT)
        rows_ref[k * ROWS_PER_EXPERT:(k + 1) * ROWS_PER_EXPERT, :] = (
            tab_ref[pl.ds(row0, ROWS_PER_EXPERT), :])


def _token_stream(gather, compute, bufs):
    nb = len(bufs)
    gather(0, bufs[0])
    gather(1, bufs[1])

    def one(t, ring_pos):
        compute(t, bufs[ring_pos % nb])
        gather(jnp.minimum(t + 2, TT - 1), bufs[(ring_pos + 2) % nb])

    def trip(i, _):
        for j in range(TOKENS_PER_TRIP):
            one(TOKENS_PER_TRIP * i + j, j)
        return 0

    n_trips = TT // TOKENS_PER_TRIP
    lax.fori_loop(0, n_trips, trip, 0)
    for t in range(n_trips * TOKENS_PER_TRIP, TT):
        one(t, t)


def _split_bf16(x):
    hi = x.astype(jnp.bfloat16).astype(jnp.float32)
    return jnp.concatenate([hi, x - hi], axis=0).astype(jnp.bfloat16)


def _expert_of_row():
    n = lax.broadcasted_iota(jnp.int32, (PEER_E * ROW_CHUNKS, PEER_E), 0) // ROW_CHUNKS
    k = lax.broadcasted_iota(jnp.int32, (PEER_E * ROW_CHUNKS, PEER_E), 1)
    return jnp.where(n == k, 1.0, 0.0).astype(jnp.bfloat16)


def _peer_act_kernel(idx_hbm, x_ref, gate_ref, tab_ref, w_ref, idx_smem, sem, z_ref, *bufs):
    slot = _fetch_indices(idx_hbm, idx_smem, sem)
    m_i = lax.broadcasted_iota(jnp.int32, (2 * ROW_CHUNKS, PEER_E * ROW_CHUNKS), 0) % ROW_CHUNKS
    n_i = lax.broadcasted_iota(jnp.int32, (2 * ROW_CHUNKS, PEER_E * ROW_CHUNKS), 1) % ROW_CHUNKS
    chunk_match = m_i == n_i

    def compute(t, rows_ref):
        x16 = _split_bf16(x_ref[t])
        u = pltpu.bitcast(rows_ref[...], jnp.bfloat16)
        r = _dot_nt(x16, u)
        z_ref[pl.ds(t, 1), :] = jnp.sum(jnp.where(chunk_match, r, 0.0), axis=0, keepdims=True)

    gather = lambda t, rows_ref: _gather_rows(slot, t, idx_smem, tab_ref, rows_ref)
    _token_stream(gather, compute, bufs)
    z_hi, z_lo = _hi_lo(z_ref[...])
    owner = _expert_of_row()
    act = _dot(z_hi, owner) + _dot(z_lo, owner)
    w_ref[...] = gate_ref[...] * jax.nn.gelu(act)


def _peer_out_kernel(idx_hbm, w_ref, tab_ref, o_ref, idx_smem, sem, w8_ref, *bufs):
    slot = _fetch_indices(idx_hbm, idx_smem, sem)
    w_hi, w_lo = _hi_lo(w_ref[...])
    owner = _expert_of_row()
    w8_ref[...] = _dot_nt(w_hi, owner) + _dot_nt(w_lo, owner)
    m_i = lax.broadcasted_iota(jnp.int32, (ROW_CHUNKS, PEER_E * ROW_CHUNKS), 0)
    n_i = lax.broadcasted_iota(jnp.int32, (ROW_CHUNKS, PEER_E * ROW_CHUNKS), 1) % ROW_CHUNKS
    chunk_match = m_i == n_i

    def compute(t, rows_ref):
        lhs = _split_bf16(jnp.where(chunk_match, w8_ref[pl.ds(t, 1), :], 0.0))
        v = pltpu.bitcast(rows_ref[...], jnp.bfloat16)
        o = _dot(lhs, v)
        o_ref[t] = o[0:ROW_CHUNKS] + o[ROW_CHUNKS:2 * ROW_CHUNKS]

    gather = lambda t, rows_ref: _gather_rows(slot, t, idx_smem, tab_ref, rows_ref)
    _token_stream(gather, compute, bufs)


def _peer_scratch():
    rows = pltpu.VMEM((PEER_E * ROWS_PER_EXPERT, LANES), jnp.int32)
    return [pltpu.SMEM((2 * IDX_TILE,), jnp.int32), pltpu.SemaphoreType.DMA((2,)),
            pltpu.VMEM((TT, PEER_E * ROW_CHUNKS), jnp.float32)] + [rows] * ROW_BUFFERS


def _peer_act(idx_tiles, x8, gates, tab_u):
    n = x8.shape[0]
    return pl.pallas_call(
        _peer_act_kernel,
        grid=(n // TT,),
        in_specs=[pl.BlockSpec(memory_space=pl.ANY),
                  pl.BlockSpec((TT, ROW_CHUNKS, LANES), lambda i: (i, 0, 0)),
                  pl.BlockSpec((TT, PEER_E), lambda i: (i, 0)),
                  pl.BlockSpec(tab_u.shape, lambda i: (0, 0))],
        out_specs=pl.BlockSpec((TT, PEER_E), lambda i: (i, 0)),
        out_shape=jax.ShapeDtypeStruct((n, PEER_E), jnp.float32),
        scratch_shapes=_peer_scratch(),
        compiler_params=_params(("arbitrary",)),
        name="peer_activations",
    )(idx_tiles, x8, gates, tab_u)


def _peer_out(idx_tiles, w, tab_v):
    n = w.shape[0]
    return pl.pallas_call(
        _peer_out_kernel,
        grid=(n // TT,),
        in_specs=[pl.BlockSpec(memory_space=pl.ANY),
                  pl.BlockSpec((TT, PEER_E), lambda i: (i, 0)),
                  pl.BlockSpec(tab_v.shape, lambda i: (0, 0))],
        out_specs=pl.BlockSpec((TT, ROW_CHUNKS, LANES), lambda i: (i, 0, 0)),
        out_shape=jax.ShapeDtypeStruct((n, ROW_CHUNKS, LANES), jnp.float32),
        scratch_shapes=_peer_scratch(),
        compiler_params=_params(("arbitrary",)),
        name="peer_combine",
    )(idx_tiles, w, tab_v)


def _pack_table(tab):
    bits = lax.bitcast_convert_type(tab.astype(jnp.bfloat16), jnp.uint16).astype(jnp.uint32)
    bits = bits.reshape(tab.shape[0] * ROWS_PER_EXPERT, 2 * LANES)
    words = bits[:, :LANES] | (bits[:, LANES:] << 16)
    return lax.bitcast_convert_type(words, jnp.int32)


def _final_kernel(x1_ref, p_ref, mod_ref, g_ref, o_ref):
    x2 = x1_ref[...] + mod_ref[0, 5:6, :] * p_ref[...]
    o_ref[...] = _rms(x2, D_MODEL) * g_ref[...]


def _final(x1, peer, mod3, g, seq):
    n = x1.shape[0]
    tm = TM_PROJ
    tps = seq // tm
    row = lambda i: (i, 0)
    return pl.pallas_call(
        _final_kernel,
        grid=(n // tm,),
        in_specs=[pl.BlockSpec((tm, D_MODEL), row), pl.BlockSpec((tm, D_MODEL), row),
                  pl.BlockSpec((1, 6, D_MODEL), lambda i: (i // tps, 0, 0)),
                  pl.BlockSpec((1, D_MODEL), lambda i: (0, 0))],
        out_specs=pl.BlockSpec((tm, D_MODEL), row),
        out_shape=jax.ShapeDtypeStruct((n, D_MODEL), jnp.float32),
        compiler_params=_params(("arbitrary",)),
        name="final_norm",
    )(x1, peer, mod3, g)


def _attn_perm():
    idx = []
    for r in range(NSA_REP):
        for g in range(NSA_KV_HEADS):
            h = g * NSA_REP + r
            idx.extend(range(h * HEAD_DIM, (h + 1) * HEAD_DIM))
    return np.asarray(idx, np.int32)


def _rel_bucket_np(dist):
    n = np.maximum(dist, 0)
    exact = REL_BUCKETS // 2
    log_ratio = (np.log(np.maximum(n, 1).astype(np.float32) / np.float32(exact))
                 / np.float32(math.log(REL_MAX_DIST / exact)))
    large = exact + (log_ratio * np.float32(REL_BUCKETS - exact)).astype(np.int32)
    return np.where(n < exact, n, np.minimum(large, REL_BUCKETS - 1)).astype(np.int32)


def _bias_tables(rel_table, seq):
    nq = seq // TQ
    ncp = seq // CMP_STRIDE

    def lookup(bucket):
        onehot = (jnp.asarray(bucket)[..., None] == jnp.arange(REL_BUCKETS)).astype(jnp.float32)
        out = jnp.einsum("...b,bh->...h", onehot, rel_table, precision=HIGHEST)
        return out.reshape(bucket.shape + (NSA_KV_HEADS, NSA_REP))
    n_near = -(-REL_MAX_DIST // TQ) + 1
    r_i = np.arange(TQ)[:, None]
    c_i = np.arange(TK)[None, :]
    near = np.stack([_rel_bucket_np(d * TQ + r_i - c_i) for d in range(n_near)]
                    + [np.full((TQ, TK), REL_BUCKETS - 1, np.int32)])
    b_near = jnp.transpose(lookup(near), (3, 0, 4, 1, 2))
    b_near = b_near.reshape(NSA_KV_HEADS, n_near + 1, NSA_REP * TQ, TK)
    t_i = np.arange(seq)[:, None]
    n_i = np.arange(ncp)[None, :]
    cmp_b = _rel_bucket_np(t_i - (n_i * CMP_STRIDE + CMP_BLK - 1))
    b_cmp = jnp.transpose(lookup(cmp_b), (2, 3, 0, 1))
    b_cmp = b_cmp.reshape(NSA_KV_HEADS, NSA_REP, nq, TQ, ncp).transpose(0, 2, 1, 3, 4)
    b_cmp = b_cmp.reshape(NSA_KV_HEADS, nq, NSA_REP * TQ, ncp)
    return b_cmp, b_near


def _cmp_to_slc(seq):
    ncp = seq // CMP_STRIDE
    n_cmp = (seq - CMP_BLK) // CMP_STRIDE + 1
    n_slc = seq // SEL_BLK
    cs = np.arange(ncp) * CMP_STRIDE
    ss = np.arange(LANES) * SEL_BLK
    ov = (cs[:, None] < ss[None, :] + SEL_BLK) & (cs[:, None] + CMP_BLK > ss[None, :])
    ov &= (np.arange(ncp)[:, None] < n_cmp) & (np.arange(LANES)[None, :] < n_slc)
    return jnp.asarray(ov.T, jnp.bfloat16)


def kernel(x, c, ln_mix_g, ln_ffn_g, w_mod, b_mod, w_in, cmp_pe_k, cmp_pe_v, cmp_wk1, cmp_wk2,
           cmp_wv1, cmp_wv2, conv_w, norm_attn_g, norm_conv_g, w_out, peer_wq, peer_keys, peer_u,
           peer_v, rel_table, ln_final_g):
    B, S, _ = x.shape
    n = B * S
    assert w_mod.shape[0] == 1 and S % TM_PROJ == 0 and S % TQ == 0 and n % TT == 0
    assert TQ == TK and WINDOW % TK == 0
    x2 = x.reshape(n, D_MODEL)
    perm = _attn_perm()

    mod3 = _modulation(c, w_mod[0], b_mod[0]).reshape(B, 6, D_MODEL)

    w = w_in[0]
    q_cols = w[:, :NSA_WIDTH][:, perm]
    kv_cols = w[:, NSA_WIDTH:NSA_WIDTH + 6 * KV_W]
    g0 = NSA_WIDTH + 6 * KV_W
    gate_cols = jnp.pad(w[:, g0:g0 + NSA_HEADS * 3], ((0, 0), (0, LANES - NSA_HEADS * 3)))
    conv_cols = w[:, g0 + NSA_HEADS * 3:]
    w_perm = jnp.concatenate([q_cols, kv_cols, gate_cols, conv_cols], axis=1).astype(jnp.bfloat16)

    q4, kc, vc, ks, vs, kw, vw, gates, conv_n = _in_projection(
        x2, mod3, ln_mix_g[0].reshape(1, D_MODEL), w_perm, conv_w[0].reshape(CONV_K, CONV_WIDTH),
        norm_conv_g[0].reshape(1, CONV_WIDTH), S)

    ncp = S // CMP_STRIDE
    tok = jnp.stack([kc, vc]).reshape(2, B, ncp, CMP_STRIDE, NSA_KV_HEADS, HEAD_DIM)
    tok = tok.transpose(0, 1, 4, 2, 3, 5).reshape(2, B, NSA_KV_HEADS, ncp, CMP_STRIDE * HEAD_DIM)
    cmp_kv = _compress(tok,
                       jnp.stack([cmp_pe_k[0], cmp_pe_v[0]]).reshape(2, 1, CMP_BLK * HEAD_DIM),
                       jnp.stack([cmp_wk1[0], cmp_wv1[0]]), jnp.stack([cmp_wk2[0], cmp_wv2[0]]))

    b_cmp, b_near = _bias_tables(rel_table, S)
    attn4 = _attention(q4, ks, vs, kw, vw, cmp_kv, gates, b_cmp, b_near, _cmp_to_slc(S),
                       norm_attn_g[0][perm].reshape(NSA_REP, 1, LANES), B, S)

    w_out_p = jnp.concatenate([w_out[0][:NSA_WIDTH][perm], w_out[0][NSA_WIDTH:]], axis=0)
    half = PEER_QDIM // 2
    keys = peer_keys[0].reshape(2 * PEER_HEADS, PEER_NKEYS, half)
    lo = jnp.pad(keys, ((0, 0), (0, 0), (0, half)))
    hi = jnp.pad(keys, ((0, 0), (0, 0), (half, 0)))
    is_hi = (jnp.arange(2 * PEER_HEADS) % 2 == 1)[:, None, None]
    keys_x = jnp.where(is_hi, hi, lo).astype(jnp.bfloat16)
    x1, h2, eid_t, gate_t = _out_proj_route(
        attn4, conv_n, x2, mod3, w_out_p.astype(jnp.bfloat16), ln_ffn_g[0].reshape(1, D_MODEL),
        peer_wq[0].astype(jnp.bfloat16), keys_x, S)

    idx_tiles = (eid_t.T * ROWS_PER_EXPERT).reshape(n // TT, TT * PEER_E)
    w_act = _peer_act(idx_tiles, h2.reshape(n, ROW_CHUNKS, LANES), gate_t.T, _pack_table(peer_u[0]))
    peer = _peer_out(idx_tiles, w_act, _pack_table(peer_v[0])).reshape(n, D_MODEL)

    out = _final(x1, peer, mod3, ln_final_g.reshape(1, D_MODEL), S)
    return out.reshape(B, S, D_MODEL)
```

```python
import functools
import math

import numpy as np
import jax
import jax.numpy as jnp
from jax import lax
from jax.experimental import pallas as pl
from jax.experimental.pallas import tpu as pltpu

D_MODEL = 1024
HEAD_DIM = 64
NSA_HEADS = 8
NSA_KV_HEADS = 2
NSA_REP = NSA_HEADS // NSA_KV_HEADS
NSA_WIDTH = NSA_HEADS * HEAD_DIM
KV_W = NSA_KV_HEADS * HEAD_DIM
CONV_WIDTH = D_MODEL - NSA_WIDTH
CONV_K = 3
CMP_BLK = 32
CMP_STRIDE = 16
SEL_BLK = 64
SEL_TOPN = 8
WINDOW = 512
FORCE_SCORE = 1e4
REL_BUCKETS = 32
REL_MAX_DIST = 128
PEER_HEADS = 8
PEER_NKEYS = 128
PEER_EXPERTS = PEER_NKEYS * PEER_NKEYS
PEER_QDIM = 128
PEER_TOPK = 16
PEER_E = PEER_HEADS * PEER_TOPK
EPS = 1e-6

LANES = 128
SUBLANES = 8
ROW_CHUNKS = D_MODEL // LANES
VMEM_LIMIT = 56 * 1024 * 1024

TM_PROJ = 512
TM_OUT = 256
TQ = 256
TK = 256
TT = 128
NEG = -1e30
HIGHEST = lax.Precision.HIGHEST

C_Q = 0
C_KV = C_Q + NSA_WIDTH
C_GATE = C_KV + 6 * KV_W
C_CONV = C_GATE + LANES
N_COLS = C_CONV + 3 * CONV_WIDTH


def _params(sem):
    return pltpu.CompilerParams(dimension_semantics=sem, vmem_limit_bytes=VMEM_LIMIT)


def _dot(a, b, precision=None):
    return jnp.dot(a, b, preferred_element_type=jnp.float32, precision=precision)


def _dot_nt(a, b, precision=None):
    return lax.dot_general(a, b, (((1,), (1,)), ((), ())),
                           preferred_element_type=jnp.float32, precision=precision)


def _rms(x, n):
    return x * lax.rsqrt(jnp.sum(x * x, axis=-1, keepdims=True) * (1.0 / n) + EPS)


def _mod_kernel(c_ref, w_ref, b_ref, o_ref):
    c = c_ref[...]
    act = c * jax.nn.sigmoid(c)
    o_ref[...] = _dot(act, w_ref[...], HIGHEST) + b_ref[...]


def _modulation(c, w_mod, b_mod):
    B = c.shape[0]
    n = w_mod.shape[1]
    bn = D_MODEL
    return pl.pallas_call(
        _mod_kernel,
        grid=(n // bn,),
        in_specs=[pl.BlockSpec((B, D_MODEL), lambda j: (0, 0)),
                  pl.BlockSpec((D_MODEL, bn), lambda j: (0, j)),
                  pl.BlockSpec((1, bn), lambda j: (0, j))],
        out_specs=pl.BlockSpec((B, bn), lambda j: (0, j)),
        out_shape=jax.ShapeDtypeStruct((B, n), jnp.float32),
        compiler_params=_params(("arbitrary",)),
        name="modulation",
    )(c, w_mod, b_mod.reshape(1, n))


def _inproj_kernel(tiles_per_seq, x_ref, mod_ref, g_ref, w_ref, cw_ref, cg_ref,
                   q_ref, kc_ref, vc_ref, ks_ref, vs_ref, kw_ref, vw_ref, gate_ref, conv_ref,
                   carry_ref):
    i = pl.program_id(0)
    x = x_ref[...]
    sh = mod_ref[0, 0:1, :]
    sc = mod_ref[0, 1:2, :]
    h = (_rms(x, D_MODEL) * g_ref[...]) * (1.0 + sc) + sh
    proj = _dot(h.astype(jnp.bfloat16), w_ref[...])

    for r in range(NSA_REP):
        q_ref[r] = (proj[:, C_Q + r * LANES:C_Q + (r + 1) * LANES] * (HEAD_DIM ** -0.5)
                    ).astype(jnp.bfloat16)
    kv = lambda k: proj[:, C_KV + k * KV_W:C_KV + (k + 1) * KV_W]
    kc_ref[...] = kv(0)
    vc_ref[...] = kv(1)
    ks_ref[...] = kv(2).astype(jnp.bfloat16)
    vs_ref[...] = kv(3).astype(jnp.bfloat16)
    kw_ref[...] = kv(4).astype(jnp.bfloat16)
    vw_ref[...] = kv(5).astype(jnp.bfloat16)
    gate_ref[...] = jax.nn.sigmoid(proj[:, C_GATE:C_GATE + LANES])

    cb = proj[:, C_CONV:C_CONV + CONV_WIDTH]
    cc = proj[:, C_CONV + CONV_WIDTH:C_CONV + 2 * CONV_WIDTH]
    ch = proj[:, C_CONV + 2 * CONV_WIDTH:C_CONV + 3 * CONV_WIDTH]
    z = cc * ch
    tm = z.shape[0]

    @pl.when(i % tiles_per_seq == 0)
    def _():
        carry_ref[...] = jnp.zeros_like(carry_ref)

    prev1 = carry_ref[SUBLANES - 1:SUBLANES, :]
    prev2 = carry_ref[SUBLANES - 2:SUBLANES - 1, :]
    row = lax.broadcasted_iota(jnp.int32, (tm, CONV_WIDTH), 0)
    z1 = jnp.where(row == 0, prev1, pltpu.roll(z, 1, 0))
    z2 = pltpu.roll(z, 2, 0)
    z2 = jnp.where(row == 0, prev2, jnp.where(row == 1, prev1, z2))
    carry_ref[...] = z[tm - SUBLANES:, :]
    y = cw_ref[0:1, :] * z2 + cw_ref[1:2, :] * z1 + cw_ref[2:3, :] * z
    conv_ref[...] = (_rms(cb * y, CONV_WIDTH) * cg_ref[...]).astype(jnp.bfloat16)


def _in_projection(x2, mod3, ln_g, w_perm, conv_w, conv_g, seq):
    n = x2.shape[0]
    tm = TM_PROJ
    tps = seq // tm
    row = lambda i: (i, 0)
    kv_f32 = jax.ShapeDtypeStruct((n, KV_W), jnp.float32)
    kv_bf = jax.ShapeDtypeStruct((n, KV_W), jnp.bfloat16)
    return pl.pallas_call(
        functools.partial(_inproj_kernel, tps),
        grid=(n // tm,),
        in_specs=[pl.BlockSpec((tm, D_MODEL), row),
                  pl.BlockSpec((1, 6, D_MODEL), lambda i: (i // tps, 0, 0)),
                  pl.BlockSpec((1, D_MODEL), lambda i: (0, 0)),
                  pl.BlockSpec((D_MODEL, N_COLS), lambda i: (0, 0)),
                  pl.BlockSpec((CONV_K, CONV_WIDTH), lambda i: (0, 0)),
                  pl.BlockSpec((1, CONV_WIDTH), lambda i: (0, 0))],
        out_specs=[pl.BlockSpec((NSA_REP, tm, LANES), lambda i: (0, i, 0))]
                  + [pl.BlockSpec((tm, KV_W), row)] * 6
                  + [pl.BlockSpec((tm, LANES), row), pl.BlockSpec((tm, CONV_WIDTH), row)],
        out_shape=[jax.ShapeDtypeStruct((NSA_REP, n, LANES), jnp.bfloat16),
                   kv_f32, kv_f32, kv_bf, kv_bf, kv_bf, kv_bf,
                   jax.ShapeDtypeStruct((n, LANES), jnp.float32),
                   jax.ShapeDtypeStruct((n, CONV_WIDTH), jnp.bfloat16)],
        scratch_shapes=[pltpu.VMEM((SUBLANES, CONV_WIDTH), jnp.float32)],
        compiler_params=_params(("arbitrary",)),
        name="in_projection",
    )(x2, mod3, ln_g, w_perm, conv_w, conv_g)


def _compress_kernel(tok_ref, pe_ref, w1_ref, w2_ref, o_ref):
    half = (CMP_BLK // 2) * HEAD_DIM
    w1a = w1_ref[0, 0:half, :]
    w1b = w1_ref[0, half:2 * half, :]
    pe_term = _dot(pe_ref[0], w1_ref[0], HIGHEST)
    ncp = tok_ref.shape[3]
    row = lax.broadcasted_iota(jnp.int32, (ncp, HEAD_DIM), 0)
    for g in range(NSA_KV_HEADS):
        t = tok_ref[0, 0, g]
        a = _dot(t, w1a, HIGHEST)
        b = _dot(t, w1b, HIGHEST)
        pre = a + pltpu.roll(b, ncp - 1, 0) + pe_term
        out = _dot(jax.nn.gelu(pre), w2_ref[0], HIGHEST)
        o_ref[0, 0, :, g * HEAD_DIM:(g + 1) * HEAD_DIM] = jnp.where(row < ncp - 1, out, 0.0)


def _compress(tok, pe, w1, w2):
    _, B, G, ncp, width = tok.shape
    return pl.pallas_call(
        _compress_kernel,
        grid=(2, B),
        in_specs=[pl.BlockSpec((1, 1, G, ncp, width), lambda k, b: (k, b, 0, 0, 0)),
                  pl.BlockSpec((1, 1, CMP_BLK * HEAD_DIM), lambda k, b: (k, 0, 0)),
                  pl.BlockSpec((1, CMP_BLK * HEAD_DIM, HEAD_DIM), lambda k, b: (k, 0, 0)),
                  pl.BlockSpec((1, HEAD_DIM, HEAD_DIM), lambda k, b: (k, 0, 0))],
        out_specs=pl.BlockSpec((1, 1, ncp, KV_W), lambda k, b: (k, b, 0, 0)),
        out_shape=jax.ShapeDtypeStruct((2, B, ncp, KV_W), jnp.float32),
        compiler_params=_params(("arbitrary", "arbitrary")),
        name="compress_kv",
    )(tok, pe, w1, w2)


M_FLOOR = -1e29


def _hi_lo(x):
    hi = x.astype(jnp.bfloat16)
    return hi, (x - hi.astype(jnp.float32)).astype(jnp.bfloat16)


def _attn_kernel(n_slc, q_ref, ks_ref, vs_ref, kw_ref, vw_ref, cmp_ref, gate_ref,
                 bcmp_ref, bnear_ref, c2s_ref, ng_ref, o_ref, og0_ref):
    qi = pl.program_id(1)
    t0 = qi * TQ
    rows = NSA_REP * TQ
    ncp = cmp_ref.shape[2]
    q4 = q_ref[...].reshape(rows, LANES)
    lane = lax.broadcasted_iota(jnp.int32, (1, LANES), 1)
    tq_col = t0 + lax.broadcasted_iota(jnp.int32, (TQ, 1), 0)
    t_col = jnp.concatenate([tq_col] * NSA_REP, axis=0)
    rowq = lax.broadcasted_iota(jnp.int32, (TQ, 1), 0)
    colk = lax.broadcasted_iota(jnp.int32, (1, TK), 1)
    gates_hi, gates_lo = _hi_lo(gate_ref[...])
    n_near = bnear_ref.shape[1] - 1

    groups = []
    for g in range(NSA_KV_HEADS):
        in_g = (lane // HEAD_DIM) == g
        q = jnp.where(in_g, q4, jnp.zeros_like(q4))

        kc_hi, kc_lo = _hi_lo(cmp_ref[0, 0])
        s = _dot_nt(q, kc_hi) + _dot_nt(q, kc_lo) + bcmp_ref[g, 0]
        n_idx = lax.broadcasted_iota(jnp.int32, (1, ncp), 1)
        valid = (n_idx * CMP_STRIDE + (CMP_BLK - 1) <= t_col) & (n_idx < ncp - 1)
        s = jnp.where(valid, s, NEG)
        e = jnp.where(valid, jnp.exp(s - jnp.max(s, axis=-1, keepdims=True)), 0.0)
        e_hi, e_lo = _hi_lo(e)
        ones = jnp.ones((ncp, ncp), jnp.bfloat16)
        den = _dot(e_hi, ones) + _dot(e_lo, ones)
        p_cmp = e / jnp.maximum(den, 1e-30)
        o_cmp = _dot(p_cmp.astype(jnp.bfloat16), cmp_ref[1, 0].astype(jnp.bfloat16))

        p_sum = p_cmp[0:TQ]
        for r in range(1, NSA_REP):
            p_sum = p_sum + p_cmp[r * TQ:(r + 1) * TQ]
        ps_hi, ps_lo = _hi_lo(p_sum)
        imp = (_dot_nt(c2s_ref[...], ps_hi) + _dot_nt(c2s_ref[...], ps_lo))[0:n_slc]
        blk = lax.broadcasted_iota(jnp.int32, (n_slc, TQ), 0)
        blk_f = blk.astype(jnp.float32)
        tok = t0 + lax.broadcasted_iota(jnp.int32, (1, TQ), 1)
        cur = tok // SEL_BLK
        forced = (blk == 0) | (blk == cur) | (blk == cur - 1)
        score = jnp.where(forced, FORCE_SCORE, jnp.where(blk * SEL_BLK <= tok, imp, -jnp.inf))
        taken = jnp.zeros((n_slc, TQ), jnp.bool_)
        for _ in range(min(SEL_TOPN, n_slc)):
            eff = jnp.where(taken, -jnp.inf, score)
            mx = jnp.max(eff, axis=0, keepdims=True)
            cand = jnp.where((eff == mx) & jnp.logical_not(taken), blk_f, float(LANES))
            taken = taken | (blk_f == jnp.min(cand, axis=0, keepdims=True))
        sel_t = jnp.concatenate([jnp.where(taken, 1.0, 0.0),
                                 jnp.zeros((LANES - n_slc, TQ), jnp.float32)], axis=0)
        sel = sel_t.T.astype(jnp.bfloat16)
        groups.append((in_g, q, o_cmp, sel))

    blk_row = lax.broadcasted_iota(jnp.int32, (LANES, TK), 0)
    blk_col = lax.broadcasted_iota(jnp.int32, (LANES, TK), 1) // SEL_BLK
    causal_add = jnp.where(rowq - colk >= 0, 0.0, NEG)
    edge_add = jnp.where(rowq - colk < 0, 0.0, NEG)

    def step(g, k, v, kt, mask_add, carry):
        in_g, q = groups[g][0], groups[g][1]
        m_old, acc_old = carry
        s = _dot_nt(q, k) + bnear_ref[g, jnp.minimum(qi - kt, n_near)]
        s = (s.reshape(NSA_REP, TQ, TK) + mask_add[None]).reshape(rows, TK)
        m_new = jnp.maximum(m_old, jnp.max(s, axis=-1, keepdims=True))
        p = jnp.exp(s - m_new).astype(jnp.bfloat16)
        v_aug = jnp.where(in_g, v, jnp.ones((TK, LANES), jnp.bfloat16))
        return m_new, jnp.exp(m_old - m_new) * acc_old + _dot(p, v_aug)

    def slc_step(kt, carry):
        at = pl.ds(pl.multiple_of(kt * TK, TK), TK)
        k, v = ks_ref[at, :], vs_ref[at, :]
        expand = jnp.where(blk_row == blk_col + kt * (TK // SEL_BLK), 1.0, 0.0).astype(jnp.bfloat16)
        diag = jnp.where(kt == qi, causal_add, 0.0)
        out = []
        for g in range(NSA_KV_HEADS):
            chosen = _dot(groups[g][3], expand)
            out.append(step(g, k, v, kt, (chosen - 1.0) * (-NEG) + diag, carry[g]))
        return tuple(out)

    def win_step(kt, carry):
        at = pl.ds(pl.multiple_of(kt * TK, TK), TK)
        k, v = kw_ref[at, :], vw_ref[at, :]
        mask_add = jnp.where(kt == qi, causal_add,
                             jnp.where(kt == qi - WINDOW // TK, edge_add, 0.0))
        return tuple(step(g, k, v, kt, mask_add, carry[g]) for g in range(NSA_KV_HEADS))

    def finish(acc):
        return acc / jnp.maximum(pltpu.roll(acc, HEAD_DIM, 1), 1e-30)

    init = ((jnp.full((rows, 1), M_FLOOR, jnp.float32), jnp.zeros((rows, LANES), jnp.float32)),
            ) * NSA_KV_HEADS
    slc = lax.fori_loop(0, qi + 1, slc_step, init)
    win = lax.fori_loop(jnp.maximum(qi - WINDOW // TK, 0), qi + 1, win_step, init)

    for g in range(NSA_KV_HEADS):
        in_g, _, o_cmp, _ = groups[g]
        o_slc = finish(slc[g][1])
        o_win = finish(win[g][1])

        n_gate = NSA_REP * 3
        src = lax.broadcasted_iota(jnp.int32, (LANES, n_gate * LANES), 0)
        dst = lax.broadcasted_iota(jnp.int32, (LANES, n_gate * LANES), 1) // LANES
        spread = jnp.where(src == g * n_gate + dst, 1.0, 0.0).astype(jnp.bfloat16)
        gate_x = _dot(gates_hi, spread) + _dot(gates_lo, spread)
        for r in range(NSA_REP):
            sl = slice(r * TQ, (r + 1) * TQ)
            gate = lambda j: gate_x[:, (r * 3 + j) * LANES:(r * 3 + j + 1) * LANES]
            o = gate(0) * o_cmp[sl] + gate(1) * o_slc[sl] + gate(2) * o_win[sl]
            if g == 0:
                og0_ref[r] = o
            else:
                og0_ref[r] = jnp.where(in_g, o, og0_ref[r])

    ss = jnp.zeros((TQ, 1), jnp.float32)
    for r in range(NSA_REP):
        o = og0_ref[r]
        ss = ss + jnp.sum(o * o, axis=-1, keepdims=True)
    inv = lax.rsqrt(ss * (1.0 / NSA_WIDTH) + EPS)
    for r in range(NSA_REP):
        o_ref[r] = (og0_ref[r] * inv * ng_ref[r]).astype(jnp.bfloat16)


def _attention(q4, ks, vs, kw, vw, cmp_kv, gates, bias_cmp, bias_near, c2s, norm_g, batch, seq):
    n = batch * seq
    nq = seq // TQ
    ncp = cmp_kv.shape[2]
    n_slc = seq // SEL_BLK
    kv_spec = pl.BlockSpec((seq, KV_W), lambda b, i: (b, 0))
    return pl.pallas_call(
        functools.partial(_attn_kernel, n_slc),
        grid=(batch, nq),
        in_specs=[pl.BlockSpec((NSA_REP, TQ, LANES), lambda b, i: (0, b * nq + i, 0)),
                  kv_spec, kv_spec, kv_spec, kv_spec,
                  pl.BlockSpec((2, 1, ncp, KV_W), lambda b, i: (0, b, 0, 0)),
                  pl.BlockSpec((TQ, LANES), lambda b, i: (b * nq + i, 0)),
                  pl.BlockSpec((NSA_KV_HEADS, 1, NSA_REP * TQ, ncp), lambda b, i: (0, i, 0, 0)),
                  pl.BlockSpec(bias_near.shape, lambda b, i: (0, 0, 0, 0)),
                  pl.BlockSpec(c2s.shape, lambda b, i: (0, 0)),
                  pl.BlockSpec((NSA_REP, 1, LANES), lambda b, i: (0, 0, 0))],
        out_specs=pl.BlockSpec((NSA_REP, TQ, LANES), lambda b, i: (0, b * nq + i, 0)),
        out_shape=jax.ShapeDtypeStruct((NSA_REP, n, LANES), jnp.bfloat16),
        scratch_shapes=[pltpu.VMEM((NSA_REP, TQ, LANES), jnp.float32)],
        compiler_params=_params(("arbitrary", "arbitrary")),
        name="sparse_attention",
    )(q4, ks, vs, kw, vw, cmp_kv, gates, bias_cmp, bias_near, c2s, norm_g)


def _topk_rows(s, k):
    nrow = s.shape[0]
    rowf = lax.broadcasted_iota(jnp.int32, s.shape, 0).astype(jnp.float32)
    slot = lax.broadcasted_iota(jnp.int32, (k, s.shape[1]), 0)
    vals = jnp.zeros((k, s.shape[1]), jnp.float32)
    idxs = jnp.zeros((k, s.shape[1]), jnp.float32)
    for j in range(k):
        m = jnp.max(s, axis=0, keepdims=True)
        first = jnp.min(jnp.where(s == m, rowf, float(nrow)), axis=0, keepdims=True)
        vals = jnp.where(slot == j, m, vals)
        idxs = jnp.where(slot == j, first, idxs)
        s = jnp.where(rowf == first, -jnp.inf, s)
    return vals, idxs


def _route_kernel(attn_ref, conv_ref, x_ref, mod_ref, wout_ref, g_ref, wq_ref, keys_ref,
                  x1_ref, h2_ref, eid_ref, gate_ref):
    mixed = jnp.concatenate([attn_ref[r] for r in range(NSA_REP)] + [conv_ref[...]], axis=-1)
    x1 = x_ref[...] + mod_ref[0, 2:3, :] * _dot(mixed, wout_ref[...])
    x1_ref[...] = x1
    h2 = (_rms(x1, D_MODEL) * g_ref[...]) * (1.0 + mod_ref[0, 4:5, :]) + mod_ref[0, 3:4, :]
    h2_ref[...] = h2
    qp = _dot(h2.astype(jnp.bfloat16), wq_ref[...]).astype(jnp.bfloat16)

    tm = qp.shape[0]
    slot = lax.broadcasted_iota(jnp.int32, (PEER_TOPK, tm), 0)
    K = PEER_TOPK
    n_mid = K // 2 - 1
    row = lax.broadcasted_iota(jnp.int32, (K + SUBLANES * n_mid + SUBLANES, tm), 0)
    mid_a = (row - K) // SUBLANES + 1
    tail = row >= K + SUBLANES * n_mid
    cand_a = jnp.where(row < K, 0, jnp.where(tail, K // 2 + row % SUBLANES, mid_a))
    cand_b = jnp.where(row < K, row, jnp.where(tail, 0, row % SUBLANES))
    in_stair = (cand_a + 1) * (cand_b + 1) <= K
    rowf = (cand_a * K + cand_b).astype(jnp.float32)

    def staircase(first_half, second_half):
        parts = [first_half[0:1] + second_half]
        parts += [first_half[a:a + 1] + second_half[0:SUBLANES] for a in range(1, n_mid + 1)]
        parts += [first_half[K // 2:K] + second_half[0:1]]
        return jnp.concatenate(parts, axis=0)

    gates_t, rows_t = [], []
    for h in range(PEER_HEADS):
        blk = qp[:, h * PEER_QDIM:(h + 1) * PEER_QDIM]
        v1, i1 = _topk_rows(_dot_nt(keys_ref[2 * h], blk), PEER_TOPK)
        v2, i2 = _topk_rows(_dot_nt(keys_ref[2 * h + 1], blk), PEER_TOPK)
        cand = jnp.where(in_stair, staircase(v1, v2), -jnp.inf)
        eid = staircase(i1 * float(PEER_NKEYS), i2)
        top_s = jnp.zeros((PEER_TOPK, tm), jnp.float32)
        top_e = jnp.zeros((PEER_TOPK, tm), jnp.float32)
        for j in range(PEER_TOPK):
            m = jnp.max(cand, axis=0, keepdims=True)
            first = jnp.min(jnp.where(cand == m, rowf, 1e9), axis=0, keepdims=True)
            hit = rowf == first
            e = jnp.sum(jnp.where(hit, eid, 0.0), axis=0, keepdims=True)
            top_s = jnp.where(slot == j, m, top_s)
            top_e = jnp.where(slot == j, e, top_e)
            cand = jnp.where(hit, -jnp.inf, cand)
        ex = jnp.exp(top_s - jnp.max(top_s, axis=0, keepdims=True))
        gates_t.append(ex / jnp.sum(ex, axis=0, keepdims=True))
        rows_t.append(top_e * float(ROWS_PER_EXPERT))
    gate_ref[...] = jnp.concatenate(gates_t, axis=0).T
    eid_ref[...] = jnp.concatenate(rows_t, axis=0).T.astype(jnp.int32)


def _out_proj_route(attn4, conv_n, x2, mod3, w_out_p, ln_g, wq, keys_x, seq):
    n = x2.shape[0]
    tm = TM_OUT
    tps = seq // tm
    row = lambda i: (i, 0)
    const2 = lambda i: (0, 0)
    return pl.pallas_call(
        _route_kernel,
        grid=(n // tm,),
        in_specs=[pl.BlockSpec((NSA_REP, tm, LANES), lambda i: (0, i, 0)),
                  pl.BlockSpec((tm, CONV_WIDTH), row),
                  pl.BlockSpec((tm, D_MODEL), row),
                  pl.BlockSpec((1, 6, D_MODEL), lambda i: (i // tps, 0, 0)),
                  pl.BlockSpec((D_MODEL, D_MODEL), const2),
                  pl.BlockSpec((1, D_MODEL), const2),
                  pl.BlockSpec((D_MODEL, PEER_HEADS * PEER_QDIM), const2),
                  pl.BlockSpec((2 * PEER_HEADS, PEER_NKEYS, PEER_QDIM), lambda i: (0, 0, 0))],
        out_specs=[pl.BlockSpec((tm, D_MODEL), row), pl.BlockSpec((tm, D_MODEL), row),
                   pl.BlockSpec((tm, PEER_E), row), pl.BlockSpec((tm, PEER_E), row)],
        out_shape=[jax.ShapeDtypeStruct((n, D_MODEL), jnp.float32),
                   jax.ShapeDtypeStruct((n, D_MODEL), jnp.float32),
                   jax.ShapeDtypeStruct((n, PEER_E), jnp.int32),
                   jax.ShapeDtypeStruct((n, PEER_E), jnp.float32)],
        compiler_params=_params(("arbitrary",)),
        name="out_proj_route",
    )(attn4, conv_n, x2, mod3, w_out_p, ln_g, wq, keys_x)


ROWS_PER_EXPERT = ROW_CHUNKS // 2


IDX_TILE = TT * PEER_E
ROW_BUFFERS = 3
TOKENS_PER_TRIP = 2 * ROW_BUFFERS


def _index_copy(idx_hbm, idx_smem, sem, tile, slot):
    dst = idx_smem.at[pl.ds(pl.multiple_of(slot * IDX_TILE, IDX_TILE), IDX_TILE)]
    return pltpu.make_async_copy(idx_hbm.at[tile], dst, sem.at[slot])


def _fetch_indices(idx_hbm, idx_smem, sem):
    i = pl.program_id(0)
    slot = i % 2

    @pl.when(i == 0)
    def _():
        _index_copy(idx_hbm, idx_smem, sem, 0, 0).start()

    _index_copy(idx_hbm, idx_smem, sem, i, slot).wait()

    @pl.when(i + 1 < pl.num_programs(0))
    def _():
        _index_copy(idx_hbm, idx_smem, sem, i + 1, 1 - slot).start()

    return slot


def _gather_rows(slot, t, idx_smem, tab_ref, rows_ref):
    base = slot * IDX_TILE + t * PEER_E
    for k in range(PEER_E):
        if k % SUBLANES == 0:
            view = idx_smem.at[pl.ds(pl.multiple_of(base + k, SUBLANES), SUBLANES)]
        e = view[k % SUBLANES]
        row0 = pl.multiple_of(e, ROWS_PER_EXPERT)
        rows_ref[k * ROWS_PER_EXPERT:(k + 1) * ROWS_PER_EXPERT, :] = (
            tab_ref[pl.ds(row0, ROWS_PER_EXPERT), :])


def _token_stream(gather, compute, bufs):
    nb = len(bufs)
    gather(0, bufs[0])
    gather(1, bufs[1])

    def one(t, ring_pos):
        compute(t, bufs[ring_pos % nb])
        gather(jnp.minimum(t + 2, TT - 1), bufs[(ring_pos + 2) % nb])

    def trip(i, _):
        for j in range(TOKENS_PER_TRIP):
            one(TOKENS_PER_TRIP * i + j, j)
        return 0

    n_trips = TT // TOKENS_PER_TRIP
    lax.fori_loop(0, n_trips, trip, 0)
    for t in range(n_trips * TOKENS_PER_TRIP, TT):
        one(t, t)


def _split_bf16(x):
    hi = x.astype(jnp.bfloat16).astype(jnp.float32)
    return jnp.concatenate([hi, x - hi], axis=0).astype(jnp.bfloat16)


def _expert_of_row():
    n = lax.broadcasted_iota(jnp.int32, (PEER_E * ROW_CHUNKS, PEER_E), 0) // ROW_CHUNKS
    k = lax.broadcasted_iota(jnp.int32, (PEER_E * ROW_CHUNKS, PEER_E), 1)
    return jnp.where(n == k, 1.0, 0.0).astype(jnp.bfloat16)


def _peer_act_kernel(idx_hbm, x_ref, gate_ref, tab_ref, w_ref, idx_smem, sem, z_ref, *bufs):
    slot = _fetch_indices(idx_hbm, idx_smem, sem)
    m_i = lax.broadcasted_iota(jnp.int32, (2 * ROW_CHUNKS, PEER_E * ROW_CHUNKS), 0) % ROW_CHUNKS
    n_i = lax.broadcasted_iota(jnp.int32, (2 * ROW_CHUNKS, PEER_E * ROW_CHUNKS), 1) % ROW_CHUNKS
    chunk_match = m_i == n_i

    def compute(t, rows_ref):
        x_row = x_ref[pl.ds(t, 1), :]
        x8 = jnp.concatenate([x_row[:, c * LANES:(c + 1) * LANES] for c in range(ROW_CHUNKS)],
                             axis=0)
        x16 = _split_bf16(x8)
        u = pltpu.bitcast(rows_ref[...], jnp.bfloat16)
        r = _dot_nt(x16, u)
        z_ref[pl.ds(t, 1), :] = jnp.sum(jnp.where(chunk_match, r, 0.0), axis=0, keepdims=True)

    gather = lambda t, rows_ref: _gather_rows(slot, t, idx_smem, tab_ref, rows_ref)
    _token_stream(gather, compute, bufs)
    z_hi, z_lo = _hi_lo(z_ref[...])
    owner = _expert_of_row()
    act = _dot(z_hi, owner) + _dot(z_lo, owner)
    w_ref[...] = gate_ref[...] * jax.nn.gelu(act)


def _peer_out_kernel(idx_hbm, w_ref, tab_ref, x1_ref, mod_ref, g_ref, o_ref,
                     idx_smem, sem, w8_ref, *bufs):
    slot = _fetch_indices(idx_hbm, idx_smem, sem)
    w_hi, w_lo = _hi_lo(w_ref[...])
    owner = _expert_of_row()
    w8_ref[...] = _dot_nt(w_hi, owner) + _dot_nt(w_lo, owner)
    m_i = lax.broadcasted_iota(jnp.int32, (ROW_CHUNKS, PEER_E * ROW_CHUNKS), 0)
    n_i = lax.broadcasted_iota(jnp.int32, (ROW_CHUNKS, PEER_E * ROW_CHUNKS), 1) % ROW_CHUNKS
    chunk_match = m_i == n_i

    def compute(t, rows_ref):
        lhs = _split_bf16(jnp.where(chunk_match, w8_ref[pl.ds(t, 1), :], 0.0))
        v = pltpu.bitcast(rows_ref[...], jnp.bfloat16)
        o = _dot(lhs, v)
        o8 = o[0:ROW_CHUNKS] + o[ROW_CHUNKS:2 * ROW_CHUNKS]
        tops = [o8 if c == 0 else pltpu.roll(o8, ROW_CHUNKS - c, 0) for c in range(ROW_CHUNKS)]
        o_ref[pl.ds(t, 1), :] = jnp.concatenate([top[0:1, :] for top in tops], axis=1)

    gather = lambda t, rows_ref: _gather_rows(slot, t, idx_smem, tab_ref, rows_ref)
    _token_stream(gather, compute, bufs)
    x2 = x1_ref[...] + mod_ref[0, 5:6, :] * o_ref[...]
    o_ref[...] = _rms(x2, D_MODEL) * g_ref[...]


def _peer_scratch():
    rows = pltpu.VMEM((PEER_E * ROWS_PER_EXPERT, LANES), jnp.int32)
    return [pltpu.SMEM((2 * IDX_TILE,), jnp.int32), pltpu.SemaphoreType.DMA((2,)),
            pltpu.VMEM((TT, PEER_E * ROW_CHUNKS), jnp.float32)] + [rows] * ROW_BUFFERS


def _peer_act(idx_tiles, x8, gates, tab_u):
    n = x8.shape[0]
    return pl.pallas_call(
        _peer_act_kernel,
        grid=(n // TT,),
        in_specs=[pl.BlockSpec(memory_space=pl.ANY),
                  pl.BlockSpec((TT, D_MODEL), lambda i: (i, 0)),
                  pl.BlockSpec((TT, PEER_E), lambda i: (i, 0)),
                  pl.BlockSpec(tab_u.shape, lambda i: (0, 0))],
        out_specs=pl.BlockSpec((TT, PEER_E), lambda i: (i, 0)),
        out_shape=jax.ShapeDtypeStruct((n, PEER_E), jnp.float32),
        scratch_shapes=_peer_scratch(),
        compiler_params=_params(("arbitrary",)),
        name="peer_activations",
    )(idx_tiles, x8, gates, tab_u)


def _peer_out(idx_tiles, w, tab_v, x1, mod3, final_g, seq):
    n = w.shape[0]
    tps = seq // TT
    return pl.pallas_call(
        _peer_out_kernel,
        grid=(n // TT,),
        in_specs=[pl.BlockSpec(memory_space=pl.ANY),
                  pl.BlockSpec((TT, PEER_E), lambda i: (i, 0)),
                  pl.BlockSpec(tab_v.shape, lambda i: (0, 0)),
                  pl.BlockSpec((TT, D_MODEL), lambda i: (i, 0)),
                  pl.BlockSpec((1, 6, D_MODEL), lambda i: (i // tps, 0, 0)),
                  pl.BlockSpec((1, D_MODEL), lambda i: (0, 0))],
        out_specs=pl.BlockSpec((TT, D_MODEL), lambda i: (i, 0)),
        out_shape=jax.ShapeDtypeStruct((n, D_MODEL), jnp.float32),
        scratch_shapes=_peer_scratch(),
        compiler_params=_params(("arbitrary",)),
        name="peer_combine",
    )(idx_tiles, w, tab_v, x1, mod3, final_g)


def _pack_table(tab):
    bits = lax.bitcast_convert_type(tab.astype(jnp.bfloat16), jnp.uint16).astype(jnp.uint32)
    bits = bits.reshape(tab.shape[0] * ROWS_PER_EXPERT, 2 * LANES)
    words = bits[:, :LANES] | (bits[:, LANES:] << 16)
    return lax.bitcast_convert_type(words, jnp.int32)


def _attn_perm():
    idx = []
    for r in range(NSA_REP):
        for g in range(NSA_KV_HEADS):
            h = g * NSA_REP + r
            idx.extend(range(h * HEAD_DIM, (h + 1) * HEAD_DIM))
    return np.asarray(idx, np.int32)


def _rel_bucket_np(dist):
    n = np.maximum(dist, 0)
    exact = REL_BUCKETS // 2
    log_ratio = (np.log(np.maximum(n, 1).astype(np.float32) / np.float32(exact))
                 / np.float32(math.log(REL_MAX_DIST / exact)))
    large = exact + (log_ratio * np.float32(REL_BUCKETS - exact)).astype(np.int32)
    return np.where(n < exact, n, np.minimum(large, REL_BUCKETS - 1)).astype(np.int32)


def _bias_tables(rel_table, seq):
    nq = seq // TQ
    ncp = seq // CMP_STRIDE

    def lookup(bucket):
        onehot = (jnp.asarray(bucket)[..., None] == jnp.arange(REL_BUCKETS)).astype(jnp.float32)
        out = jnp.einsum("...b,bh->...h", onehot, rel_table, precision=HIGHEST)
        return out.reshape(bucket.shape + (NSA_KV_HEADS, NSA_REP))
    n_near = -(-REL_MAX_DIST // TQ) + 1
    r_i = np.arange(TQ)[:, None]
    c_i = np.arange(TK)[None, :]
    near = np.stack([_rel_bucket_np(d * TQ + r_i - c_i) for d in range(n_near)]
                    + [np.full((TQ, TK), REL_BUCKETS - 1, np.int32)])
    b_near = jnp.transpose(lookup(near), (3, 0, 4, 1, 2))
    b_near = b_near.reshape(NSA_KV_HEADS, n_near + 1, NSA_REP * TQ, TK)
    t_i = np.arange(seq)[:, None]
    n_i = np.arange(ncp)[None, :]
    cmp_b = _rel_bucket_np(t_i - (n_i * CMP_STRIDE + CMP_BLK - 1))
    b_cmp = jnp.transpose(lookup(cmp_b), (2, 3, 0, 1))
    b_cmp = b_cmp.reshape(NSA_KV_HEADS, NSA_REP, nq, TQ, ncp).transpose(0, 2, 1, 3, 4)
    b_cmp = b_cmp.reshape(NSA_KV_HEADS, nq, NSA_REP * TQ, ncp)
    return b_cmp, b_near


def _cmp_to_slc(seq):
    ncp = seq // CMP_STRIDE
    n_cmp = (seq - CMP_BLK) // CMP_STRIDE + 1
    n_slc = seq // SEL_BLK
    cs = np.arange(ncp) * CMP_STRIDE
    ss = np.arange(LANES) * SEL_BLK
    ov = (cs[:, None] < ss[None, :] + SEL_BLK) & (cs[:, None] + CMP_BLK > ss[None, :])
    ov &= (np.arange(ncp)[:, None] < n_cmp) & (np.arange(LANES)[None, :] < n_slc)
    return jnp.asarray(ov.T, jnp.bfloat16)


def kernel(x, c, ln_mix_g, ln_ffn_g, w_mod, b_mod, w_in, cmp_pe_k, cmp_pe_v, cmp_wk1, cmp_wk2,
           cmp_wv1, cmp_wv2, conv_w, norm_attn_g, norm_conv_g, w_out, peer_wq, peer_keys, peer_u,
           peer_v, rel_table, ln_final_g):
    B, S, _ = x.shape
    n = B * S
    assert w_mod.shape[0] == 1 and S % TM_PROJ == 0 and S % TQ == 0 and n % TT == 0
    assert TQ == TK and WINDOW % TK == 0
    x2 = x.reshape(n, D_MODEL)
    perm = _attn_perm()

    mod3 = _modulation(c, w_mod[0], b_mod[0]).reshape(B, 6, D_MODEL)

    w = w_in[0]
    q_cols = w[:, :NSA_WIDTH][:, perm]
    kv_cols = w[:, NSA_WIDTH:NSA_WIDTH + 6 * KV_W]
    g0 = NSA_WIDTH + 6 * KV_W
    gate_cols = jnp.pad(w[:, g0:g0 + NSA_HEADS * 3], ((0, 0), (0, LANES - NSA_HEADS * 3)))
    conv_cols = w[:, g0 + NSA_HEADS * 3:]
    w_perm = jnp.concatenate([q_cols, kv_cols, gate_cols, conv_cols], axis=1).astype(jnp.bfloat16)

    q4, kc, vc, ks, vs, kw, vw, gates, conv_n = _in_projection(
        x2, mod3, ln_mix_g[0].reshape(1, D_MODEL), w_perm, conv_w[0].reshape(CONV_K, CONV_WIDTH),
        norm_conv_g[0].reshape(1, CONV_WIDTH), S)

    ncp = S // CMP_STRIDE
    tok = jnp.stack([kc, vc]).reshape(2, B, ncp, CMP_STRIDE, NSA_KV_HEADS, HEAD_DIM)
    tok = tok.transpose(0, 1, 4, 2, 3, 5).reshape(2, B, NSA_KV_HEADS, ncp, CMP_STRIDE * HEAD_DIM)
    cmp_kv = _compress(tok,
                       jnp.stack([cmp_pe_k[0], cmp_pe_v[0]]).reshape(2, 1, CMP_BLK * HEAD_DIM),
                       jnp.stack([cmp_wk1[0], cmp_wv1[0]]), jnp.stack([cmp_wk2[0], cmp_wv2[0]]))

    b_cmp, b_near = _bias_tables(rel_table, S)
    attn4 = _attention(q4, ks, vs, kw, vw, cmp_kv, gates, b_cmp, b_near, _cmp_to_slc(S),
                       norm_attn_g[0][perm].reshape(NSA_REP, 1, LANES), B, S)

    w_out_p = jnp.concatenate([w_out[0][:NSA_WIDTH][perm], w_out[0][NSA_WIDTH:]], axis=0)
    half = PEER_QDIM // 2
    keys = peer_keys[0].reshape(2 * PEER_HEADS, PEER_NKEYS, half)
    lo = jnp.pad(keys, ((0, 0), (0, 0), (0, half)))
    hi = jnp.pad(keys, ((0, 0), (0, 0), (half, 0)))
    is_hi = (jnp.arange(2 * PEER_HEADS) % 2 == 1)[:, None, None]
    keys_x = jnp.where(is_hi, hi, lo).astype(jnp.bfloat16)
    x1, h2, idx_rows, peer_gates = _out_proj_route(
        attn4, conv_n, x2, mod3, w_out_p.astype(jnp.bfloat16), ln_ffn_g[0].reshape(1, D_MODEL),
        peer_wq[0].astype(jnp.bfloat16), keys_x, S)

    idx_tiles = idx_rows.reshape(n // TT, TT * PEER_E)
    w_act = _peer_act(idx_tiles, h2, peer_gates, _pack_table(peer_u[0]))
    out = _peer_out(idx_tiles, w_act, _pack_table(peer_v[0]), x1, mod3,
                    ln_final_g.reshape(1, D_MODEL), S)
    return out.reshape(B, S, D_MODEL)
```

```python
import functools
import math

import numpy as np
import jax
import jax.numpy as jnp
from jax import lax
from jax.experimental import pallas as pl
from jax.experimental.pallas import tpu as pltpu

D_MODEL = 1024
HEAD_DIM = 64
NSA_HEADS = 8
NSA_KV_HEADS = 2
NSA_REP = NSA_HEADS // NSA_KV_HEADS
NSA_WIDTH = NSA_HEADS * HEAD_DIM
KV_W = NSA_KV_HEADS * HEAD_DIM
CONV_WIDTH = D_MODEL - NSA_WIDTH
CONV_K = 3
CMP_BLK = 32
CMP_STRIDE = 16
SEL_BLK = 64
SEL_TOPN = 8
WINDOW = 512
FORCE_SCORE = 1e4
REL_BUCKETS = 32
REL_MAX_DIST = 128
PEER_HEADS = 8
PEER_NKEYS = 128
PEER_EXPERTS = PEER_NKEYS * PEER_NKEYS
PEER_QDIM = 128
PEER_TOPK = 16
PEER_E = PEER_HEADS * PEER_TOPK
EPS = 1e-6

LANES = 128
SUBLANES = 8
ROW_CHUNKS = D_MODEL // LANES
VMEM_LIMIT = 56 * 1024 * 1024

TM_PROJ = 512
TM_OUT = 256
TQ = 256
TK = 256
TT = 128
NEG = -1e30
HIGHEST = lax.Precision.HIGHEST

C_Q = 0
C_KV = C_Q + NSA_WIDTH
C_GATE = C_KV + 6 * KV_W
C_CONV = C_GATE + LANES
N_COLS = C_CONV + 3 * CONV_WIDTH


def _params(sem):
    return pltpu.CompilerParams(dimension_semantics=sem, vmem_limit_bytes=VMEM_LIMIT)


def _dot(a, b, precision=None):
    return jnp.dot(a, b, preferred_element_type=jnp.float32, precision=precision)


def _dot_nt(a, b, precision=None):
    return lax.dot_general(a, b, (((1,), (1,)), ((), ())),
                           preferred_element_type=jnp.float32, precision=precision)


def _rms(x, n):
    return x * lax.rsqrt(jnp.sum(x * x, axis=-1, keepdims=True) * (1.0 / n) + EPS)


def _mod_kernel(c_ref, w_ref, b_ref, o_ref):
    c = c_ref[...]
    act = c * jax.nn.sigmoid(c)
    o_ref[...] = _dot(act, w_ref[...], HIGHEST) + b_ref[...]


def _modulation(c, w_mod, b_mod):
    B = c.shape[0]
    n = w_mod.shape[1]
    bn = D_MODEL
    return pl.pallas_call(
        _mod_kernel,
        grid=(n // bn,),
        in_specs=[pl.BlockSpec((B, D_MODEL), lambda j: (0, 0)),
                  pl.BlockSpec((D_MODEL, bn), lambda j: (0, j)),
                  pl.BlockSpec((1, bn), lambda j: (0, j))],
        out_specs=pl.BlockSpec((B, bn), lambda j: (0, j)),
        out_shape=jax.ShapeDtypeStruct((B, n), jnp.float32),
        compiler_params=_params(("arbitrary",)),
        name="modulation",
    )(c, w_mod, b_mod.reshape(1, n))


def _inproj_kernel(tiles_per_seq, x_ref, mod_ref, g_ref, w_ref, cw_ref, cg_ref,
                   q_ref, kc_ref, vc_ref, ks_ref, vs_ref, kw_ref, vw_ref, gate_ref, conv_ref,
                   carry_ref):
    i = pl.program_id(0)
    x = x_ref[...]
    sh = mod_ref[0, 0:1, :]
    sc = mod_ref[0, 1:2, :]
    h = (_rms(x, D_MODEL) * g_ref[...]) * (1.0 + sc) + sh
    proj = _dot(h.astype(jnp.bfloat16), w_ref[...])

    for r in range(NSA_REP):
        q_ref[r] = (proj[:, C_Q + r * LANES:C_Q + (r + 1) * LANES] * (HEAD_DIM ** -0.5)
                    ).astype(jnp.bfloat16)
    kv = lambda k: proj[:, C_KV + k * KV_W:C_KV + (k + 1) * KV_W]
    kc_ref[...] = kv(0)
    vc_ref[...] = kv(1)
    ks_ref[...] = kv(2).astype(jnp.bfloat16)
    vs_ref[...] = kv(3).astype(jnp.bfloat16)
    kw_ref[...] = kv(4).astype(jnp.bfloat16)
    vw_ref[...] = kv(5).astype(jnp.bfloat16)
    gate_ref[...] = jax.nn.sigmoid(proj[:, C_GATE:C_GATE + LANES])

    cb = proj[:, C_CONV:C_CONV + CONV_WIDTH]
    cc = proj[:, C_CONV + CONV_WIDTH:C_CONV + 2 * CONV_WIDTH]
    ch = proj[:, C_CONV + 2 * CONV_WIDTH:C_CONV + 3 * CONV_WIDTH]
    z = cc * ch
    tm = z.shape[0]

    @pl.when(i % tiles_per_seq == 0)
    def _():
        carry_ref[...] = jnp.zeros_like(carry_ref)

    prev1 = carry_ref[SUBLANES - 1:SUBLANES, :]
    prev2 = carry_ref[SUBLANES - 2:SUBLANES - 1, :]
    row = lax.broadcasted_iota(jnp.int32, (tm, CONV_WIDTH), 0)
    z1 = jnp.where(row == 0, prev1, pltpu.roll(z, 1, 0))
    z2 = pltpu.roll(z, 2, 0)
    z2 = jnp.where(row == 0, prev2, jnp.where(row == 1, prev1, z2))
    carry_ref[...] = z[tm - SUBLANES:, :]
    y = cw_ref[0:1, :] * z2 + cw_ref[1:2, :] * z1 + cw_ref[2:3, :] * z
    conv_ref[...] = (_rms(cb * y, CONV_WIDTH) * cg_ref[...]).astype(jnp.bfloat16)


def _in_projection(x2, mod3, ln_g, w_perm, conv_w, conv_g, seq):
    n = x2.shape[0]
    tm = TM_PROJ
    tps = seq // tm
    row = lambda i: (i, 0)
    kv_f32 = jax.ShapeDtypeStruct((n, KV_W), jnp.float32)
    kv_bf = jax.ShapeDtypeStruct((n, KV_W), jnp.bfloat16)
    return pl.pallas_call(
        functools.partial(_inproj_kernel, tps),
        grid=(n // tm,),
        in_specs=[pl.BlockSpec((tm, D_MODEL), row),
                  pl.BlockSpec((1, 6, D_MODEL), lambda i: (i // tps, 0, 0)),
                  pl.BlockSpec((1, D_MODEL), lambda i: (0, 0)),
                  pl.BlockSpec((D_MODEL, N_COLS), lambda i: (0, 0)),
                  pl.BlockSpec((CONV_K, CONV_WIDTH), lambda i: (0, 0)),
                  pl.BlockSpec((1, CONV_WIDTH), lambda i: (0, 0))],
        out_specs=[pl.BlockSpec((NSA_REP, tm, LANES), lambda i: (0, i, 0))]
                  + [pl.BlockSpec((tm, KV_W), row)] * 6
                  + [pl.BlockSpec((tm, LANES), row), pl.BlockSpec((tm, CONV_WIDTH), row)],
        out_shape=[jax.ShapeDtypeStruct((NSA_REP, n, LANES), jnp.bfloat16),
                   kv_f32, kv_f32, kv_bf, kv_bf, kv_bf, kv_bf,
                   jax.ShapeDtypeStruct((n, LANES), jnp.float32),
                   jax.ShapeDtypeStruct((n, CONV_WIDTH), jnp.bfloat16)],
        scratch_shapes=[pltpu.VMEM((SUBLANES, CONV_WIDTH), jnp.float32)],
        compiler_params=_params(("arbitrary",)),
        name="in_projection",
    )(x2, mod3, ln_g, w_perm, conv_w, conv_g)


def _compress_kernel(tok_ref, pe_ref, w1_ref, w2_ref, o_ref):
    half = (CMP_BLK // 2) * HEAD_DIM
    w1a = w1_ref[0, 0:half, :]
    w1b = w1_ref[0, half:2 * half, :]
    pe_term = _dot(pe_ref[0], w1_ref[0], HIGHEST)
    ncp = tok_ref.shape[3]
    row = lax.broadcasted_iota(jnp.int32, (ncp, HEAD_DIM), 0)
    for g in range(NSA_KV_HEADS):
        t = tok_ref[0, 0, g]
        a = _dot(t, w1a, HIGHEST)
        b = _dot(t, w1b, HIGHEST)
        pre = a + pltpu.roll(b, ncp - 1, 0) + pe_term
        out = _dot(jax.nn.gelu(pre), w2_ref[0], HIGHEST)
        o_ref[0, 0, :, g * HEAD_DIM:(g + 1) * HEAD_DIM] = jnp.where(row < ncp - 1, out, 0.0)


def _compress(tok, pe, w1, w2):
    _, B, G, ncp, width = tok.shape
    return pl.pallas_call(
        _compress_kernel,
        grid=(2, B),
        in_specs=[pl.BlockSpec((1, 1, G, ncp, width), lambda k, b: (k, b, 0, 0, 0)),
                  pl.BlockSpec((1, 1, CMP_BLK * HEAD_DIM), lambda k, b: (k, 0, 0)),
                  pl.BlockSpec((1, CMP_BLK * HEAD_DIM, HEAD_DIM), lambda k, b: (k, 0, 0)),
                  pl.BlockSpec((1, HEAD_DIM, HEAD_DIM), lambda k, b: (k, 0, 0))],
        out_specs=pl.BlockSpec((1, 1, ncp, KV_W), lambda k, b: (k, b, 0, 0)),
        out_shape=jax.ShapeDtypeStruct((2, B, ncp, KV_W), jnp.float32),
        compiler_params=_params(("arbitrary", "arbitrary")),
        name="compress_kv",
    )(tok, pe, w1, w2)


M_FLOOR = -1e29


def _hi_lo(x):
    hi = x.astype(jnp.bfloat16)
    return hi, (x - hi.astype(jnp.float32)).astype(jnp.bfloat16)


def _attn_kernel(n_slc, q_ref, ks_ref, vs_ref, kw_ref, vw_ref, cmp_ref, gate_ref,
                 bcmp_ref, bnear_ref, c2s_ref, ng_ref, o_ref, og0_ref):
    qi = pl.program_id(1)
    t0 = qi * TQ
    rows = NSA_REP * TQ
    ncp = cmp_ref.shape[2]
    q4 = q_ref[...].reshape(rows, LANES)
    lane = lax.broadcasted_iota(jnp.int32, (1, LANES), 1)
    tq_col = t0 + lax.broadcasted_iota(jnp.int32, (TQ, 1), 0)
    t_col = jnp.concatenate([tq_col] * NSA_REP, axis=0)
    rowq = lax.broadcasted_iota(jnp.int32, (TQ, 1), 0)
    colk = lax.broadcasted_iota(jnp.int32, (1, TK), 1)
    gates_hi, gates_lo = _hi_lo(gate_ref[...])
    n_near = bnear_ref.shape[1] - 1

    groups = []
    for g in range(NSA_KV_HEADS):
        in_g = (lane // HEAD_DIM) == g
        q = jnp.where(in_g, q4, jnp.zeros_like(q4))

        kc_hi, kc_lo = _hi_lo(cmp_ref[0, 0])
        s = _dot_nt(q, kc_hi) + _dot_nt(q, kc_lo) + bcmp_ref[g, 0]
        n_idx = lax.broadcasted_iota(jnp.int32, (1, ncp), 1)
        valid = (n_idx * CMP_STRIDE + (CMP_BLK - 1) <= t_col) & (n_idx < ncp - 1)
        s = jnp.where(valid, s, NEG)
        e = jnp.where(valid, jnp.exp(s - jnp.max(s, axis=-1, keepdims=True)), 0.0)
        e_hi, e_lo = _hi_lo(e)
        ones = jnp.ones((ncp, ncp), jnp.bfloat16)
        den = _dot(e_hi, ones) + _dot(e_lo, ones)
        p_cmp = e / jnp.maximum(den, 1e-30)
        o_cmp = _dot(p_cmp.astype(jnp.bfloat16), cmp_ref[1, 0].astype(jnp.bfloat16))

        p_sum = p_cmp[0:TQ]
        for r in range(1, NSA_REP):
            p_sum = p_sum + p_cmp[r * TQ:(r + 1) * TQ]
        ps_hi, ps_lo = _hi_lo(p_sum)
        imp = (_dot_nt(c2s_ref[...], ps_hi) + _dot_nt(c2s_ref[...], ps_lo))[0:n_slc]
        blk = lax.broadcasted_iota(jnp.int32, (n_slc, TQ), 0)
        blk_f = blk.astype(jnp.float32)
        tok = t0 + lax.broadcasted_iota(jnp.int32, (1, TQ), 1)
        cur = tok // SEL_BLK
        forced = (blk == 0) | (blk == cur) | (blk == cur - 1)
        score = jnp.where(forced, FORCE_SCORE, jnp.where(blk * SEL_BLK <= tok, imp, -jnp.inf))
        taken = jnp.zeros((n_slc, TQ), jnp.bool_)
        for _ in range(min(SEL_TOPN, n_slc)):
            eff = jnp.where(taken, -jnp.inf, score)
            mx = jnp.max(eff, axis=0, keepdims=True)
            cand = jnp.where((eff == mx) & jnp.logical_not(taken), blk_f, float(LANES))
            taken = taken | (blk_f == jnp.min(cand, axis=0, keepdims=True))
        sel_t = jnp.concatenate([jnp.where(taken, 1.0, 0.0),
                                 jnp.zeros((LANES - n_slc, TQ), jnp.float32)], axis=0)
        sel = sel_t.T.astype(jnp.bfloat16)
        groups.append((in_g, q, o_cmp, sel))

    blk_row = lax.broadcasted_iota(jnp.int32, (LANES, TK), 0)
    blk_col = lax.broadcasted_iota(jnp.int32, (LANES, TK), 1) // SEL_BLK
    causal_add = jnp.where(rowq - colk >= 0, 0.0, NEG)
    edge_add = jnp.where(rowq - colk < 0, 0.0, NEG)

    def step(g, k, v, kt, mask_add, carry):
        in_g, q = groups[g][0], groups[g][1]
        m_old, acc_old = carry
        s = _dot_nt(q, k) + bnear_ref[g, jnp.minimum(qi - kt, n_near)]
        s = (s.reshape(NSA_REP, TQ, TK) + mask_add[None]).reshape(rows, TK)
        m_new = jnp.maximum(m_old, jnp.max(s, axis=-1, keepdims=True))
        p = jnp.exp(s - m_new).astype(jnp.bfloat16)
        v_aug = jnp.where(in_g, v, jnp.ones((TK, LANES), jnp.bfloat16))
        return m_new, jnp.exp(m_old - m_new) * acc_old + _dot(p, v_aug)

    def slc_step(kt, carry):
        at = pl.ds(pl.multiple_of(kt * TK, TK), TK)
        k, v = ks_ref[at, :], vs_ref[at, :]
        expand = jnp.where(blk_row == blk_col + kt * (TK // SEL_BLK), 1.0, 0.0).astype(jnp.bfloat16)
        diag = jnp.where(kt == qi, causal_add, 0.0)
        out = []
        for g in range(NSA_KV_HEADS):
            chosen = _dot(groups[g][3], expand)
            out.append(step(g, k, v, kt, (chosen - 1.0) * (-NEG) + diag, carry[g]))
        return tuple(out)

    def win_step(kt, carry):
        at = pl.ds(pl.multiple_of(kt * TK, TK), TK)
        k, v = kw_ref[at, :], vw_ref[at, :]
        mask_add = jnp.where(kt == qi, causal_add,
                             jnp.where(kt == qi - WINDOW // TK, edge_add, 0.0))
        return tuple(step(g, k, v, kt, mask_add, carry[g]) for g in range(NSA_KV_HEADS))

    def finish(acc):
        return acc / jnp.maximum(pltpu.roll(acc, HEAD_DIM, 1), 1e-30)

    init = ((jnp.full((rows, 1), M_FLOOR, jnp.float32), jnp.zeros((rows, LANES), jnp.float32)),
            ) * NSA_KV_HEADS
    slc = lax.fori_loop(0, qi + 1, slc_step, init)
    win = lax.fori_loop(jnp.maximum(qi - WINDOW // TK, 0), qi + 1, win_step, init)

    for g in range(NSA_KV_HEADS):
        in_g, _, o_cmp, _ = groups[g]
        o_slc = finish(slc[g][1])
        o_win = finish(win[g][1])

        n_gate = NSA_REP * 3
        src = lax.broadcasted_iota(jnp.int32, (LANES, n_gate * LANES), 0)
        dst = lax.broadcasted_iota(jnp.int32, (LANES, n_gate * LANES), 1) // LANES
        spread = jnp.where(src == g * n_gate + dst, 1.0, 0.0).astype(jnp.bfloat16)
        gate_x = _dot(gates_hi, spread) + _dot(gates_lo, spread)
        for r in range(NSA_REP):
            sl = slice(r * TQ, (r + 1) * TQ)
            gate = lambda j: gate_x[:, (r * 3 + j) * LANES:(r * 3 + j + 1) * LANES]
            o = gate(0) * o_cmp[sl] + gate(1) * o_slc[sl] + gate(2) * o_win[sl]
            if g == 0:
                og0_ref[r] = o
            else:
                og0_ref[r] = jnp.where(in_g, o, og0_ref[r])

    ss = jnp.zeros((TQ, 1), jnp.float32)
    for r in range(NSA_REP):
        o = og0_ref[r]
        ss = ss + jnp.sum(o * o, axis=-1, keepdims=True)
    inv = lax.rsqrt(ss * (1.0 / NSA_WIDTH) + EPS)
    for r in range(NSA_REP):
        o_ref[r] = (og0_ref[r] * inv * ng_ref[r]).astype(jnp.bfloat16)


def _attention(q4, ks, vs, kw, vw, cmp_kv, gates, bias_cmp, bias_near, c2s, norm_g, batch, seq):
    n = batch * seq
    nq = seq // TQ
    ncp = cmp_kv.shape[2]
    n_slc = seq // SEL_BLK
    kv_spec = pl.BlockSpec((seq, KV_W), lambda b, i: (b, 0))
    return pl.pallas_call(
        functools.partial(_attn_kernel, n_slc),
        grid=(batch, nq),
        in_specs=[pl.BlockSpec((NSA_REP, TQ, LANES), lambda b, i: (0, b * nq + i, 0)),
                  kv_spec, kv_spec, kv_spec, kv_spec,
                  pl.BlockSpec((2, 1, ncp, KV_W), lambda b, i: (0, b, 0, 0)),
                  pl.BlockSpec((TQ, LANES), lambda b, i: (b * nq + i, 0)),
                  pl.BlockSpec((NSA_KV_HEADS, 1, NSA_REP * TQ, ncp), lambda b, i: (0, i, 0, 0)),
                  pl.BlockSpec(bias_near.shape, lambda b, i: (0, 0, 0, 0)),
                  pl.BlockSpec(c2s.shape, lambda b, i: (0, 0)),
                  pl.BlockSpec((NSA_REP, 1, LANES), lambda b, i: (0, 0, 0))],
        out_specs=pl.BlockSpec((NSA_REP, TQ, LANES), lambda b, i: (0, b * nq + i, 0)),
        out_shape=jax.ShapeDtypeStruct((NSA_REP, n, LANES), jnp.bfloat16),
        scratch_shapes=[pltpu.VMEM((NSA_REP, TQ, LANES), jnp.float32)],
        compiler_params=_params(("arbitrary", "arbitrary")),
        name="sparse_attention",
    )(q4, ks, vs, kw, vw, cmp_kv, gates, bias_cmp, bias_near, c2s, norm_g)


def _topk_rows(s, k):
    nrow = s.shape[0]
    rowf = lax.broadcasted_iota(jnp.int32, s.shape, 0).astype(jnp.float32)
    slot = lax.broadcasted_iota(jnp.int32, (k, s.shape[1]), 0)
    vals = jnp.zeros((k, s.shape[1]), jnp.float32)
    idxs = jnp.zeros((k, s.shape[1]), jnp.float32)
    for j in range(k):
        m = jnp.max(s, axis=0, keepdims=True)
        first = jnp.min(jnp.where(s == m, rowf, float(nrow)), axis=0, keepdims=True)
        vals = jnp.where(slot == j, m, vals)
        idxs = jnp.where(slot == j, first, idxs)
        s = jnp.where(rowf == first, -jnp.inf, s)
    return vals, idxs


def _route_kernel(attn_ref, conv_ref, x_ref, mod_ref, wout_ref, g_ref, wq_ref, keys_ref,
                  x1_ref, h2_ref, eid_ref, gate_ref):
    mixed = jnp.concatenate([attn_ref[r] for r in range(NSA_REP)] + [conv_ref[...]], axis=-1)
    x1 = x_ref[...] + mod_ref[0, 2:3, :] * _dot(mixed, wout_ref[...])
    x1_ref[...] = x1
    h2 = (_rms(x1, D_MODEL) * g_ref[...]) * (1.0 + mod_ref[0, 4:5, :]) + mod_ref[0, 3:4, :]
    h2_ref[...] = h2
    qp = _dot(h2.astype(jnp.bfloat16), wq_ref[...]).astype(jnp.bfloat16)

    tm = qp.shape[0]
    slot = lax.broadcasted_iota(jnp.int32, (PEER_TOPK, tm), 0)
    K = PEER_TOPK
    n_mid = K // 2 - 1
    row = lax.broadcasted_iota(jnp.int32, (K + SUBLANES * n_mid + SUBLANES, tm), 0)
    mid_a = (row - K) // SUBLANES + 1
    tail = row >= K + SUBLANES * n_mid
    cand_a = jnp.where(row < K, 0, jnp.where(tail, K // 2 + row % SUBLANES, mid_a))
    cand_b = jnp.where(row < K, row, jnp.where(tail, 0, row % SUBLANES))
    in_stair = (cand_a + 1) * (cand_b + 1) <= K
    rowf = (cand_a * K + cand_b).astype(jnp.float32)

    def staircase(first_half, second_half):
        parts = [first_half[0:1] + second_half]
        parts += [first_half[a:a + 1] + second_half[0:SUBLANES] for a in range(1, n_mid + 1)]
        parts += [first_half[K // 2:K] + second_half[0:1]]
        return jnp.concatenate(parts, axis=0)

    gates_t, rows_t = [], []
    for h in range(PEER_HEADS):
        blk = qp[:, h * PEER_QDIM:(h + 1) * PEER_QDIM]
        v1, i1 = _topk_rows(_dot_nt(keys_ref[2 * h], blk), PEER_TOPK)
        v2, i2 = _topk_rows(_dot_nt(keys_ref[2 * h + 1], blk), PEER_TOPK)
        cand = jnp.where(in_stair, staircase(v1, v2), -jnp.inf)
        eid = staircase(i1 * float(PEER_NKEYS), i2)
        top_s = jnp.zeros((PEER_TOPK, tm), jnp.float32)
        top_e = jnp.zeros((PEER_TOPK, tm), jnp.float32)
        for j in range(PEER_TOPK):
            m = jnp.max(cand, axis=0, keepdims=True)
            first = jnp.min(jnp.where(cand == m, rowf, 1e9), axis=0, keepdims=True)
            hit = rowf == first
            e = jnp.sum(jnp.where(hit, eid, 0.0), axis=0, keepdims=True)
            top_s = jnp.where(slot == j, m, top_s)
            top_e = jnp.where(slot == j, e, top_e)
            cand = jnp.where(hit, -jnp.inf, cand)
        ex = jnp.exp(top_s - jnp.max(top_s, axis=0, keepdims=True))
        gates_t.append(ex / jnp.sum(ex, axis=0, keepdims=True))
        rows_t.append(top_e * float(ROWS_PER_EXPERT))
    gate_ref[...] = jnp.concatenate(gates_t, axis=0).T
    eid_ref[...] = jnp.concatenate(rows_t, axis=0).T.astype(jnp.int32)


def _out_proj_route(attn4, conv_n, x2, mod3, w_out_p, ln_g, wq, keys_x, seq):
    n = x2.shape[0]
    tm = TM_OUT
    tps = seq // tm
    row = lambda i: (i, 0)
    const2 = lambda i: (0, 0)
    return pl.pallas_call(
        _route_kernel,
        grid=(n // tm,),
        in_specs=[pl.BlockSpec((NSA_REP, tm, LANES), lambda i: (0, i, 0)),
                  pl.BlockSpec((tm, CONV_WIDTH), row),
                  pl.BlockSpec((tm, D_MODEL), row),
                  pl.BlockSpec((1, 6, D_MODEL), lambda i: (i // tps, 0, 0)),
                  pl.BlockSpec((D_MODEL, D_MODEL), const2),
                  pl.BlockSpec((1, D_MODEL), const2),
                  pl.BlockSpec((D_MODEL, PEER_HEADS * PEER_QDIM), const2),
                  pl.BlockSpec((2 * PEER_HEADS, PEER_NKEYS, PEER_QDIM), lambda i: (0, 0, 0))],
        out_specs=[pl.BlockSpec((tm, D_MODEL), row), pl.BlockSpec((tm, D_MODEL), row),
                   pl.BlockSpec((tm, PEER_E), row), pl.BlockSpec((tm, PEER_E), row)],
        out_shape=[jax.ShapeDtypeStruct((n, D_MODEL), jnp.float32),
                   jax.ShapeDtypeStruct((n, D_MODEL), jnp.float32),
                   jax.ShapeDtypeStruct((n, PEER_E), jnp.int32),
                   jax.ShapeDtypeStruct((n, PEER_E), jnp.float32)],
        compiler_params=_params(("arbitrary",)),
        name="out_proj_route",
    )(attn4, conv_n, x2, mod3, w_out_p, ln_g, wq, keys_x)


ROWS_PER_EXPERT = ROW_CHUNKS // 2


IDX_TILE = TT * PEER_E
ROW_BUFFERS = 4
TOKENS_PER_TRIP = 32 * ROW_BUFFERS


def _index_copy(idx_hbm, idx_smem, sem, tile, slot):
    dst = idx_smem.at[pl.ds(pl.multiple_of(slot * IDX_TILE, IDX_TILE), IDX_TILE)]
    return pltpu.make_async_copy(idx_hbm.at[tile], dst, sem.at[slot])


def _fetch_indices(idx_hbm, idx_smem, sem):
    i = pl.program_id(0)
    slot = i % 2

    @pl.when(i == 0)
    def _():
        _index_copy(idx_hbm, idx_smem, sem, 0, 0).start()

    _index_copy(idx_hbm, idx_smem, sem, i, slot).wait()

    @pl.when(i + 1 < pl.num_programs(0))
    def _():
        _index_copy(idx_hbm, idx_smem, sem, i + 1, 1 - slot).start()

    return slot


def _gather_rows(slot, t, idx_smem, tab_ref, rows_ref):
    base = slot * IDX_TILE + t * PEER_E
    for k in range(PEER_E):
        if k % SUBLANES == 0:
            view = idx_smem.at[pl.ds(pl.multiple_of(base + k, SUBLANES), SUBLANES)]
        e = view[k % SUBLANES]
        row0 = pl.multiple_of(e, ROWS_PER_EXPERT)
        rows_ref[k * ROWS_PER_EXPERT:(k + 1) * ROWS_PER_EXPERT, :] = (
            tab_ref[pl.ds(row0, ROWS_PER_EXPERT), :])


def _token_stream(gather, compute, bufs):
    nb = len(bufs)
    gather(0, bufs[0])
    gather(1, bufs[1])

    def one(t, ring_pos):
        compute(t, bufs[ring_pos % nb])
        gather(jnp.minimum(t + 2, TT - 1), bufs[(ring_pos + 2) % nb])

    def trip(i, _):
        for j in range(TOKENS_PER_TRIP):
            one(TOKENS_PER_TRIP * i + j, j)
        return 0

    n_trips = TT // TOKENS_PER_TRIP
    lax.fori_loop(0, n_trips, trip, 0)
    for t in range(n_trips * TOKENS_PER_TRIP, TT):
        one(t, t)


def _split_bf16(x):
    hi = x.astype(jnp.bfloat16).astype(jnp.float32)
    return jnp.concatenate([hi, x - hi], axis=0).astype(jnp.bfloat16)


def _expert_of_row():
    n = lax.broadcasted_iota(jnp.int32, (PEER_E * ROW_CHUNKS, PEER_E), 0) // ROW_CHUNKS
    k = lax.broadcasted_iota(jnp.int32, (PEER_E * ROW_CHUNKS, PEER_E), 1)
    return jnp.where(n == k, 1.0, 0.0).astype(jnp.bfloat16)


def _peer_act_kernel(idx_hbm, x_ref, gate_ref, tab_ref, w_ref, idx_smem, sem, z_ref, *bufs):
    slot = _fetch_indices(idx_hbm, idx_smem, sem)
    m_i = lax.broadcasted_iota(jnp.int32, (2 * ROW_CHUNKS, PEER_E * ROW_CHUNKS), 0) % ROW_CHUNKS
    n_i = lax.broadcasted_iota(jnp.int32, (2 * ROW_CHUNKS, PEER_E * ROW_CHUNKS), 1) % ROW_CHUNKS
    chunk_match = m_i == n_i

    def compute(t, rows_ref):
        x_row = x_ref[pl.ds(t, 1), :]
        x8 = jnp.concatenate([x_row[:, c * LANES:(c + 1) * LANES] for c in range(ROW_CHUNKS)],
                             axis=0)
        x16 = _split_bf16(x8)
        u = pltpu.bitcast(rows_ref[...], jnp.bfloat16)
        r = _dot_nt(x16, u)
        z_ref[pl.ds(t, 1), :] = jnp.sum(jnp.where(chunk_match, r, 0.0), axis=0, keepdims=True)

    gather = lambda t, rows_ref: _gather_rows(slot, t, idx_smem, tab_ref, rows_ref)
    _token_stream(gather, compute, bufs)
    z_hi, z_lo = _hi_lo(z_ref[...])
    owner = _expert_of_row()
    act = _dot(z_hi, owner) + _dot(z_lo, owner)
    w_ref[...] = gate_ref[...] * jax.nn.gelu(act)


def _peer_out_kernel(idx_hbm, w_ref, tab_ref, x1_ref, mod_ref, g_ref, o_ref,
                     idx_smem, sem, w8_ref, *bufs):
    slot = _fetch_indices(idx_hbm, idx_smem, sem)
    w_hi, w_lo = _hi_lo(w_ref[...])
    owner = _expert_of_row()
    w8_ref[...] = _dot_nt(w_hi, owner) + _dot_nt(w_lo, owner)
    m_i = lax.broadcasted_iota(jnp.int32, (ROW_CHUNKS, PEER_E * ROW_CHUNKS), 0)
    n_i = lax.broadcasted_iota(jnp.int32, (ROW_CHUNKS, PEER_E * ROW_CHUNKS), 1) % ROW_CHUNKS
    chunk_match = m_i == n_i

    def compute(t, rows_ref):
        lhs = _split_bf16(jnp.where(chunk_match, w8_ref[pl.ds(t, 1), :], 0.0))
        v = pltpu.bitcast(rows_ref[...], jnp.bfloat16)
        o = _dot(lhs, v)
        o8 = o[0:ROW_CHUNKS] + o[ROW_CHUNKS:2 * ROW_CHUNKS]
        tops = [o8 if c == 0 else pltpu.roll(o8, ROW_CHUNKS - c, 0) for c in range(ROW_CHUNKS)]
        o_ref[pl.ds(t, 1), :] = jnp.concatenate([top[0:1, :] for top in tops], axis=1)

    gather = lambda t, rows_ref: _gather_rows(slot, t, idx_smem, tab_ref, rows_ref)
    _token_stream(gather, compute, bufs)
    x2 = x1_ref[...] + mod_ref[0, 5:6, :] * o_ref[...]
    o_ref[...] = _rms(x2, D_MODEL) * g_ref[...]


def _peer_scratch():
    rows = pltpu.VMEM((PEER_E * ROWS_PER_EXPERT, LANES), jnp.int32)
    return [pltpu.SMEM((2 * IDX_TILE,), jnp.int32), pltpu.SemaphoreType.DMA((2,)),
            pltpu.VMEM((TT, PEER_E * ROW_CHUNKS), jnp.float32)] + [rows] * ROW_BUFFERS


def _peer_act(idx_tiles, x8, gates, tab_u):
    n = x8.shape[0]
    return pl.pallas_call(
        _peer_act_kernel,
        grid=(n // TT,),
        in_specs=[pl.BlockSpec(memory_space=pl.ANY),
                  pl.BlockSpec((TT, D_MODEL), lambda i: (i, 0)),
                  pl.BlockSpec((TT, PEER_E), lambda i: (i, 0)),
                  pl.BlockSpec(tab_u.shape, lambda i: (0, 0))],
        out_specs=pl.BlockSpec((TT, PEER_E), lambda i: (i, 0)),
        out_shape=jax.ShapeDtypeStruct((n, PEER_E), jnp.float32),
        scratch_shapes=_peer_scratch(),
        compiler_params=_params(("arbitrary",)),
        name="peer_activations",
    )(idx_tiles, x8, gates, tab_u)


def _peer_out(idx_tiles, w, tab_v, x1, mod3, final_g, seq):
    n = w.shape[0]
    tps = seq // TT
    return pl.pallas_call(
        _peer_out_kernel,
        grid=(n // TT,),
        in_specs=[pl.BlockSpec(memory_space=pl.ANY),
                  pl.BlockSpec((TT, PEER_E), lambda i: (i, 0)),
                  pl.BlockSpec(tab_v.shape, lambda i: (0, 0)),
                  pl.BlockSpec((TT, D_MODEL), lambda i: (i, 0)),
                  pl.BlockSpec((1, 6, D_MODEL), lambda i: (i // tps, 0, 0)),
                  pl.BlockSpec((1, D_MODEL), lambda i: (0, 0))],
        out_specs=pl.BlockSpec((TT, D_MODEL), lambda i: (i, 0)),
        out_shape=jax.ShapeDtypeStruct((n, D_MODEL), jnp.float32),
        scratch_shapes=_peer_scratch(),
        compiler_params=_params(("arbitrary",)),
        name="peer_combine",
    )(idx_tiles, w, tab_v, x1, mod3, final_g)


def _pack_table(tab):
    bits = lax.bitcast_convert_type(tab.astype(jnp.bfloat16), jnp.uint16).astype(jnp.uint32)
    bits = bits.reshape(tab.shape[0] * ROWS_PER_EXPERT, 2 * LANES)
    words = bits[:, :LANES] | (bits[:, LANES:] << 16)
    return lax.bitcast_convert_type(words, jnp.int32)


def _attn_perm():
    idx = []
    for r in range(NSA_REP):
        for g in range(NSA_KV_HEADS):
            h = g * NSA_REP + r
            idx.extend(range(h * HEAD_DIM, (h + 1) * HEAD_DIM))
    return np.asarray(idx, np.int32)


def _rel_bucket_np(dist):
    n = np.maximum(dist, 0)
    exact = REL_BUCKETS // 2
    log_ratio = (np.log(np.maximum(n, 1).astype(np.float32) / np.float32(exact))
                 / np.float32(math.log(REL_MAX_DIST / exact)))
    large = exact + (log_ratio * np.float32(REL_BUCKETS - exact)).astype(np.int32)
    return np.where(n < exact, n, np.minimum(large, REL_BUCKETS - 1)).astype(np.int32)


def _bias_tables(rel_table, seq):
    nq = seq // TQ
    ncp = seq // CMP_STRIDE

    def lookup(bucket):
        onehot = (jnp.asarray(bucket)[..., None] == jnp.arange(REL_BUCKETS)).astype(jnp.float32)
        out = jnp.einsum("...b,bh->...h", onehot, rel_table, precision=HIGHEST)
        return out.reshape(bucket.shape + (NSA_KV_HEADS, NSA_REP))
    n_near = -(-REL_MAX_DIST // TQ) + 1
    r_i = np.arange(TQ)[:, None]
    c_i = np.arange(TK)[None, :]
    near = np.stack([_rel_bucket_np(d * TQ + r_i - c_i) for d in range(n_near)]
                    + [np.full((TQ, TK), REL_BUCKETS - 1, np.int32)])
    b_near = jnp.transpose(lookup(near), (3, 0, 4, 1, 2))
    b_near = b_near.reshape(NSA_KV_HEADS, n_near + 1, NSA_REP * TQ, TK)
    t_i = np.arange(seq)[:, None]
    n_i = np.arange(ncp)[None, :]
    cmp_b = _rel_bucket_np(t_i - (n_i * CMP_STRIDE + CMP_BLK - 1))
    b_cmp = jnp.transpose(lookup(cmp_b), (2, 3, 0, 1))
    b_cmp = b_cmp.reshape(NSA_KV_HEADS, NSA_REP, nq, TQ, ncp).transpose(0, 2, 1, 3, 4)
    b_cmp = b_cmp.reshape(NSA_KV_HEADS, nq, NSA_REP * TQ, ncp)
    return b_cmp, b_near


def _cmp_to_slc(seq):
    ncp = seq // CMP_STRIDE
    n_cmp = (seq - CMP_BLK) // CMP_STRIDE + 1
    n_slc = seq // SEL_BLK
    cs = np.arange(ncp) * CMP_STRIDE
    ss = np.arange(LANES) * SEL_BLK
    ov = (cs[:, None] < ss[None, :] + SEL_BLK) & (cs[:, None] + CMP_BLK > ss[None, :])
    ov &= (np.arange(ncp)[:, None] < n_cmp) & (np.arange(LANES)[None, :] < n_slc)
    return jnp.asarray(ov.T, jnp.bfloat16)


def kernel(x, c, ln_mix_g, ln_ffn_g, w_mod, b_mod, w_in, cmp_pe_k, cmp_pe_v, cmp_wk1, cmp_wk2,
           cmp_wv1, cmp_wv2, conv_w, norm_attn_g, norm_conv_g, w_out, peer_wq, peer_keys, peer_u,
           peer_v, rel_table, ln_final_g):
    B, S, _ = x.shape
    n = B * S
    assert w_mod.shape[0] == 1 and S % TM_PROJ == 0 and S % TQ == 0 and n % TT == 0
    assert TQ == TK and WINDOW % TK == 0
    x2 = x.reshape(n, D_MODEL)
    perm = _attn_perm()

    mod3 = _modulation(c, w_mod[0], b_mod[0]).reshape(B, 6, D_MODEL)

    w = w_in[0]
    q_cols = w[:, :NSA_WIDTH][:, perm]
    kv_cols = w[:, NSA_WIDTH:NSA_WIDTH + 6 * KV_W]
    g0 = NSA_WIDTH + 6 * KV_W
    gate_cols = jnp.pad(w[:, g0:g0 + NSA_HEADS * 3], ((0, 0), (0, LANES - NSA_HEADS * 3)))
    conv_cols = w[:, g0 + NSA_HEADS * 3:]
    w_perm = jnp.concatenate([q_cols, kv_cols, gate_cols, conv_cols], axis=1).astype(jnp.bfloat16)

    q4, kc, vc, ks, vs, kw, vw, gates, conv_n = _in_projection(
        x2, mod3, ln_mix_g[0].reshape(1, D_MODEL), w_perm, conv_w[0].reshape(CONV_K, CONV_WIDTH),
        norm_conv_g[0].reshape(1, CONV_WIDTH), S)

    ncp = S // CMP_STRIDE
    tok = jnp.stack([kc, vc]).reshape(2, B, ncp, CMP_STRIDE, NSA_KV_HEADS, HEAD_DIM)
    tok = tok.transpose(0, 1, 4, 2, 3, 5).reshape(2, B, NSA_KV_HEADS, ncp, CMP_STRIDE * HEAD_DIM)
    cmp_kv = _compress(tok,
                       jnp.stack([cmp_pe_k[0], cmp_pe_v[0]]).reshape(2, 1, CMP_BLK * HEAD_DIM),
                       jnp.stack([cmp_wk1[0], cmp_wv1[0]]), jnp.stack([cmp_wk2[0], cmp_wv2[0]]))

    b_cmp, b_near = _bias_tables(rel_table, S)
    attn4 = _attention(q4, ks, vs, kw, vw, cmp_kv, gates, b_cmp, b_near, _cmp_to_slc(S),
                       norm_attn_g[0][perm].reshape(NSA_REP, 1, LANES), B, S)

    w_out_p = jnp.concatenate([w_out[0][:NSA_WIDTH][perm], w_out[0][NSA_WIDTH:]], axis=0)
    half = PEER_QDIM // 2
    keys = peer_keys[0].reshape(2 * PEER_HEADS, PEER_NKEYS, half)
    lo = jnp.pad(keys, ((0, 0), (0, 0), (0, half)))
    hi = jnp.pad(keys, ((0, 0), (0, 0), (half, 0)))
    is_hi = (jnp.arange(2 * PEER_HEADS) % 2 == 1)[:, None, None]
    keys_x = jnp.where(is_hi, hi, lo).astype(jnp.bfloat16)
    x1, h2, idx_rows, peer_gates = _out_proj_route(
        attn4, conv_n, x2, mod3, w_out_p.astype(jnp.bfloat16), ln_ffn_g[0].reshape(1, D_MODEL),
        peer_wq[0].astype(jnp.bfloat16), keys_x, S)

    idx_tiles = idx_rows.reshape(n // TT, TT * PEER_E)
    w_act = _peer_act(idx_tiles, h2, peer_gates, _pack_table(peer_u[0]))
    out = _peer_out(idx_tiles, w_act, _pack_table(peer_v[0]), x1, mod3,
                    ln_final_g.reshape(1, D_MODEL), S)
    return out.reshape(B, S, D_MODEL)
```

```python
import functools
import math

import numpy as np
import jax
import jax.numpy as jnp
from jax import lax
from jax.experimental import pallas as pl
from jax.experimental.pallas import tpu as pltpu

D_MODEL = 1024
HEAD_DIM = 64
NSA_HEADS = 8
NSA_KV_HEADS = 2
NSA_REP = NSA_HEADS // NSA_KV_HEADS
NSA_WIDTH = NSA_HEADS * HEAD_DIM
KV_W = NSA_KV_HEADS * HEAD_DIM
CONV_WIDTH = D_MODEL - NSA_WIDTH
CONV_K = 3
CMP_BLK = 32
CMP_STRIDE = 16
SEL_BLK = 64
SEL_TOPN = 8
WINDOW = 512
FORCE_SCORE = 1e4
REL_BUCKETS = 32
REL_MAX_DIST = 128
PEER_HEADS = 8
PEER_NKEYS = 128
PEER_EXPERTS = PEER_NKEYS * PEER_NKEYS
PEER_QDIM = 128
PEER_TOPK = 16
PEER_E = PEER_HEADS * PEER_TOPK
EPS = 1e-6

LANES = 128
SUBLANES = 8
ROW_CHUNKS = D_MODEL // LANES
VMEM_LIMIT = 56 * 1024 * 1024

TM_PROJ = 512
TM_OUT = 256
TQ = 256
TK = 256
TT = 128
NEG = -1e30
HIGHEST = lax.Precision.HIGHEST

C_Q = 0
C_KV = C_Q + NSA_WIDTH
C_GATE = C_KV + 6 * KV_W
C_CONV = C_GATE + LANES
N_COLS = C_CONV + 3 * CONV_WIDTH


def _params(sem):
    return pltpu.CompilerParams(dimension_semantics=sem, vmem_limit_bytes=VMEM_LIMIT)


def _dot(a, b, precision=None):
    return jnp.dot(a, b, preferred_element_type=jnp.float32, precision=precision)


def _dot_nt(a, b, precision=None):
    return lax.dot_general(a, b, (((1,), (1,)), ((), ())),
                           preferred_element_type=jnp.float32, precision=precision)


def _rms(x, n):
    return x * lax.rsqrt(jnp.sum(x * x, axis=-1, keepdims=True) * (1.0 / n) + EPS)


def _mod_kernel(c_ref, w_ref, b_ref, o_ref):
    c = c_ref[...]
    act = c * jax.nn.sigmoid(c)
    o_ref[...] = _dot(act, w_ref[...], HIGHEST) + b_ref[...]


def _modulation(c, w_mod, b_mod):
    B = c.shape[0]
    n = w_mod.shape[1]
    bn = D_MODEL
    return pl.pallas_call(
        _mod_kernel,
        grid=(n // bn,),
        in_specs=[pl.BlockSpec((B, D_MODEL), lambda j: (0, 0)),
                  pl.BlockSpec((D_MODEL, bn), lambda j: (0, j)),
                  pl.BlockSpec((1, bn), lambda j: (0, j))],
        out_specs=pl.BlockSpec((B, bn), lambda j: (0, j)),
        out_shape=jax.ShapeDtypeStruct((B, n), jnp.float32),
        compiler_params=_params(("arbitrary",)),
        name="modulation",
    )(c, w_mod, b_mod.reshape(1, n))


def _inproj_kernel(tiles_per_seq, x_ref, mod_ref, g_ref, w_ref, cw_ref, cg_ref,
                   q_ref, kc_ref, vc_ref, ks_ref, vs_ref, kw_ref, vw_ref, gate_ref, conv_ref,
                   carry_ref):
    i = pl.program_id(0)
    x = x_ref[...]
    sh = mod_ref[0, 0:1, :]
    sc = mod_ref[0, 1:2, :]
    h = (_rms(x, D_MODEL) * g_ref[...]) * (1.0 + sc) + sh
    proj = _dot(h.astype(jnp.bfloat16), w_ref[...])

    for r in range(NSA_REP):
        q_ref[r] = (proj[:, C_Q + r * LANES:C_Q + (r + 1) * LANES] * (HEAD_DIM ** -0.5)
                    ).astype(jnp.bfloat16)
    kv = lambda k: proj[:, C_KV + k * KV_W:C_KV + (k + 1) * KV_W]
    kc_ref[...] = kv(0)
    vc_ref[...] = kv(1)
    ks_ref[...] = kv(2).astype(jnp.bfloat16)
    vs_ref[...] = kv(3).astype(jnp.bfloat16)
    kw_ref[...] = kv(4).astype(jnp.bfloat16)
    vw_ref[...] = kv(5).astype(jnp.bfloat16)
    gate_ref[...] = jax.nn.sigmoid(proj[:, C_GATE:C_GATE + LANES])

    cb = proj[:, C_CONV:C_CONV + CONV_WIDTH]
    cc = proj[:, C_CONV + CONV_WIDTH:C_CONV + 2 * CONV_WIDTH]
    ch = proj[:, C_CONV + 2 * CONV_WIDTH:C_CONV + 3 * CONV_WIDTH]
    z = cc * ch
    tm = z.shape[0]

    @pl.when(i % tiles_per_seq == 0)
    def _():
        carry_ref[...] = jnp.zeros_like(carry_ref)

    prev1 = carry_ref[SUBLANES - 1:SUBLANES, :]
    prev2 = carry_ref[SUBLANES - 2:SUBLANES - 1, :]
    row = lax.broadcasted_iota(jnp.int32, (tm, CONV_WIDTH), 0)
    z1 = jnp.where(row == 0, prev1, pltpu.roll(z, 1, 0))
    z2 = pltpu.roll(z, 2, 0)
    z2 = jnp.where(row == 0, prev2, jnp.where(row == 1, prev1, z2))
    carry_ref[...] = z[tm - SUBLANES:, :]
    y = cw_ref[0:1, :] * z2 + cw_ref[1:2, :] * z1 + cw_ref[2:3, :] * z
    conv_ref[...] = (_rms(cb * y, CONV_WIDTH) * cg_ref[...]).astype(jnp.bfloat16)


def _in_projection(x2, mod3, ln_g, w_perm, conv_w, conv_g, seq):
    n = x2.shape[0]
    tm = TM_PROJ
    tps = seq // tm
    row = lambda i: (i, 0)
    kv_f32 = jax.ShapeDtypeStruct((n, KV_W), jnp.float32)
    kv_bf = jax.ShapeDtypeStruct((n, KV_W), jnp.bfloat16)
    return pl.pallas_call(
        functools.partial(_inproj_kernel, tps),
        grid=(n // tm,),
        in_specs=[pl.BlockSpec((tm, D_MODEL), row),
                  pl.BlockSpec((1, 6, D_MODEL), lambda i: (i // tps, 0, 0)),
                  pl.BlockSpec((1, D_MODEL), lambda i: (0, 0)),
                  pl.BlockSpec((D_MODEL, N_COLS), lambda i: (0, 0)),
                  pl.BlockSpec((CONV_K, CONV_WIDTH), lambda i: (0, 0)),
                  pl.BlockSpec((1, CONV_WIDTH), lambda i: (0, 0))],
        out_specs=[pl.BlockSpec((NSA_REP, tm, LANES), lambda i: (0, i, 0))]
                  + [pl.BlockSpec((tm, KV_W), row)] * 6
                  + [pl.BlockSpec((tm, LANES), row), pl.BlockSpec((tm, CONV_WIDTH), row)],
        out_shape=[jax.ShapeDtypeStruct((NSA_REP, n, LANES), jnp.bfloat16),
                   kv_f32, kv_f32, kv_bf, kv_bf, kv_bf, kv_bf,
                   jax.ShapeDtypeStruct((n, LANES), jnp.float32),
                   jax.ShapeDtypeStruct((n, CONV_WIDTH), jnp.bfloat16)],
        scratch_shapes=[pltpu.VMEM((SUBLANES, CONV_WIDTH), jnp.float32)],
        compiler_params=_params(("arbitrary",)),
        name="in_projection",
    )(x2, mod3, ln_g, w_perm, conv_w, conv_g)


def _compress_kernel(tok_ref, pe_ref, w1_ref, w2_ref, o_ref):
    half = (CMP_BLK // 2) * HEAD_DIM
    w1a = w1_ref[0, 0:half, :]
    w1b = w1_ref[0, half:2 * half, :]
    pe_term = _dot(pe_ref[0], w1_ref[0], HIGHEST)
    ncp = tok_ref.shape[3]
    row = lax.broadcasted_iota(jnp.int32, (ncp, HEAD_DIM), 0)
    for g in range(NSA_KV_HEADS):
        t = tok_ref[0, 0, g]
        a = _dot(t, w1a, HIGHEST)
        b = _dot(t, w1b, HIGHEST)
        pre = a + pltpu.roll(b, ncp - 1, 0) + pe_term
        out = _dot(jax.nn.gelu(pre), w2_ref[0], HIGHEST)
        o_ref[0, 0, :, g * HEAD_DIM:(g + 1) * HEAD_DIM] = jnp.where(row < ncp - 1, out, 0.0)


def _compress(tok, pe, w1, w2):
    _, B, G, ncp, width = tok.shape
    return pl.pallas_call(
        _compress_kernel,
        grid=(2, B),
        in_specs=[pl.BlockSpec((1, 1, G, ncp, width), lambda k, b: (k, b, 0, 0, 0)),
                  pl.BlockSpec((1, 1, CMP_BLK * HEAD_DIM), lambda k, b: (k, 0, 0)),
                  pl.BlockSpec((1, CMP_BLK * HEAD_DIM, HEAD_DIM), lambda k, b: (k, 0, 0)),
                  pl.BlockSpec((1, HEAD_DIM, HEAD_DIM), lambda k, b: (k, 0, 0))],
        out_specs=pl.BlockSpec((1, 1, ncp, KV_W), lambda k, b: (k, b, 0, 0)),
        out_shape=jax.ShapeDtypeStruct((2, B, ncp, KV_W), jnp.float32),
        compiler_params=_params(("arbitrary", "arbitrary")),
        name="compress_kv",
    )(tok, pe, w1, w2)


M_FLOOR = -1e29


def _hi_lo(x):
    hi = x.astype(jnp.bfloat16)
    return hi, (x - hi.astype(jnp.float32)).astype(jnp.bfloat16)


def _attn_kernel(n_slc, q_ref, ks_ref, vs_ref, kw_ref, vw_ref, cmp_ref, gate_ref,
                 bcmp_ref, bnear_ref, c2s_ref, ng_ref, o_ref, og0_ref):
    qi = pl.program_id(1)
    t0 = qi * TQ
    rows = NSA_REP * TQ
    ncp = cmp_ref.shape[2]
    q4 = q_ref[...].reshape(rows, LANES)
    lane = lax.broadcasted_iota(jnp.int32, (1, LANES), 1)
    tq_col = t0 + lax.broadcasted_iota(jnp.int32, (TQ, 1), 0)
    t_col = jnp.concatenate([tq_col] * NSA_REP, axis=0)
    rowq = lax.broadcasted_iota(jnp.int32, (TQ, 1), 0)
    colk = lax.broadcasted_iota(jnp.int32, (1, TK), 1)
    gates_hi, gates_lo = _hi_lo(gate_ref[...])
    n_near = bnear_ref.shape[1] - 1

    groups = []
    for g in range(NSA_KV_HEADS):
        in_g = (lane // HEAD_DIM) == g
        q = jnp.where(in_g, q4, jnp.zeros_like(q4))

        kc_hi, kc_lo = _hi_lo(cmp_ref[0, 0])
        s = _dot_nt(q, kc_hi) + _dot_nt(q, kc_lo) + bcmp_ref[g, 0]
        n_idx = lax.broadcasted_iota(jnp.int32, (1, ncp), 1)
        valid = (n_idx * CMP_STRIDE + (CMP_BLK - 1) <= t_col) & (n_idx < ncp - 1)
        s = jnp.where(valid, s, NEG)
        e = jnp.where(valid, jnp.exp(s - jnp.max(s, axis=-1, keepdims=True)), 0.0)
        e_hi, e_lo = _hi_lo(e)
        ones = jnp.ones((ncp, ncp), jnp.bfloat16)
        den = _dot(e_hi, ones) + _dot(e_lo, ones)
        p_cmp = e / jnp.maximum(den, 1e-30)
        o_cmp = _dot(p_cmp.astype(jnp.bfloat16), cmp_ref[1, 0].astype(jnp.bfloat16))

        p_sum = p_cmp[0:TQ]
        for r in range(1, NSA_REP):
            p_sum = p_sum + p_cmp[r * TQ:(r + 1) * TQ]
        ps_hi, ps_lo = _hi_lo(p_sum)
        imp = (_dot_nt(c2s_ref[...], ps_hi) + _dot_nt(c2s_ref[...], ps_lo))[0:n_slc]
        blk = lax.broadcasted_iota(jnp.int32, (n_slc, TQ), 0)
        blk_f = blk.astype(jnp.float32)
        tok = t0 + lax.broadcasted_iota(jnp.int32, (1, TQ), 1)
        cur = tok // SEL_BLK
        forced = (blk == 0) | (blk == cur) | (blk == cur - 1)
        score = jnp.where(forced, FORCE_SCORE, jnp.where(blk * SEL_BLK <= tok, imp, -jnp.inf))
        taken = jnp.zeros((n_slc, TQ), jnp.bool_)
        for _ in range(min(SEL_TOPN, n_slc)):
            eff = jnp.where(taken, -jnp.inf, score)
            mx = jnp.max(eff, axis=0, keepdims=True)
            cand = jnp.where((eff == mx) & jnp.logical_not(taken), blk_f, float(LANES))
            taken = taken | (blk_f == jnp.min(cand, axis=0, keepdims=True))
        sel_t = jnp.concatenate([jnp.where(taken, 1.0, 0.0),
                                 jnp.zeros((LANES - n_slc, TQ), jnp.float32)], axis=0)
        sel = sel_t.T.astype(jnp.bfloat16)
        groups.append((in_g, q, o_cmp, sel))

    blk_row = lax.broadcasted_iota(jnp.int32, (LANES, TK), 0)
    blk_col = lax.broadcasted_iota(jnp.int32, (LANES, TK), 1) // SEL_BLK
    causal_add = jnp.where(rowq - colk >= 0, 0.0, NEG)
    edge_add = jnp.where(rowq - colk < 0, 0.0, NEG)

    def step(g, k, v, kt, mask_add, carry):
        in_g, q = groups[g][0], groups[g][1]
        m_old, acc_old = carry
        s = _dot_nt(q, k) + bnear_ref[g, jnp.minimum(qi - kt, n_near)]
        s = (s.reshape(NSA_REP, TQ, TK) + mask_add[None]).reshape(rows, TK)
        m_new = jnp.maximum(m_old, jnp.max(s, axis=-1, keepdims=True))
        p = jnp.exp(s - m_new).astype(jnp.bfloat16)
        v_aug = jnp.where(in_g, v, jnp.ones((TK, LANES), jnp.bfloat16))
        return m_new, jnp.exp(m_old - m_new) * acc_old + _dot(p, v_aug)

    def slc_step(kt, carry):
        at = pl.ds(pl.multiple_of(kt * TK, TK), TK)
        k, v = ks_ref[at, :], vs_ref[at, :]
        expand = jnp.where(blk_row == blk_col + kt * (TK // SEL_BLK), 1.0, 0.0).astype(jnp.bfloat16)
        diag = jnp.where(kt == qi, causal_add, 0.0)
        out = []
        for g in range(NSA_KV_HEADS):
            chosen = _dot(groups[g][3], expand)
            out.append(step(g, k, v, kt, (chosen - 1.0) * (-NEG) + diag, carry[g]))
        return tuple(out)

    def win_step(kt, carry):
        at = pl.ds(pl.multiple_of(kt * TK, TK), TK)
        k, v = kw_ref[at, :], vw_ref[at, :]
        mask_add = jnp.where(kt == qi, causal_add,
                             jnp.where(kt == qi - WINDOW // TK, edge_add, 0.0))
        return tuple(step(g, k, v, kt, mask_add, carry[g]) for g in range(NSA_KV_HEADS))

    def finish(acc):
        return acc / jnp.maximum(pltpu.roll(acc, HEAD_DIM, 1), 1e-30)

    init = ((jnp.full((rows, 1), M_FLOOR, jnp.float32), jnp.zeros((rows, LANES), jnp.float32)),
            ) * NSA_KV_HEADS
    slc = lax.fori_loop(0, qi + 1, slc_step, init)
    win = lax.fori_loop(jnp.maximum(qi - WINDOW // TK, 0), qi + 1, win_step, init)

    for g in range(NSA_KV_HEADS):
        in_g, _, o_cmp, _ = groups[g]
        o_slc = finish(slc[g][1])
        o_win = finish(win[g][1])

        n_gate = NSA_REP * 3
        src = lax.broadcasted_iota(jnp.int32, (LANES, n_gate * LANES), 0)
        dst = lax.broadcasted_iota(jnp.int32, (LANES, n_gate * LANES), 1) // LANES
        spread = jnp.where(src == g * n_gate + dst, 1.0, 0.0).astype(jnp.bfloat16)
        gate_x = _dot(gates_hi, spread) + _dot(gates_lo, spread)
        for r in range(NSA_REP):
            sl = slice(r * TQ, (r + 1) * TQ)
            gate = lambda j: gate_x[:, (r * 3 + j) * LANES:(r * 3 + j + 1) * LANES]
            o = gate(0) * o_cmp[sl] + gate(1) * o_slc[sl] + gate(2) * o_win[sl]
            if g == 0:
                og0_ref[r] = o
            else:
                og0_ref[r] = jnp.where(in_g, o, og0_ref[r])

    ss = jnp.zeros((TQ, 1), jnp.float32)
    for r in range(NSA_REP):
        o = og0_ref[r]
        ss = ss + jnp.sum(o * o, axis=-1, keepdims=True)
    inv = lax.rsqrt(ss * (1.0 / NSA_WIDTH) + EPS)
    for r in range(NSA_REP):
        o_ref[r] = (og0_ref[r] * inv * ng_ref[r]).astype(jnp.bfloat16)


def _attention(q4, ks, vs, kw, vw, cmp_kv, gates, bias_cmp, bias_near, c2s, norm_g, batch, seq):
    n = batch * seq
    nq = seq // TQ
    ncp = cmp_kv.shape[2]
    n_slc = seq // SEL_BLK
    kv_spec = pl.BlockSpec((seq, KV_W), lambda b, i: (b, 0))
    return pl.pallas_call(
        functools.partial(_attn_kernel, n_slc),
        grid=(batch, nq),
        in_specs=[pl.BlockSpec((NSA_REP, TQ, LANES), lambda b, i: (0, b * nq + i, 0)),
                  kv_spec, kv_spec, kv_spec, kv_spec,
                  pl.BlockSpec((2, 1, ncp, KV_W), lambda b, i: (0, b, 0, 0)),
                  pl.BlockSpec((TQ, LANES), lambda b, i: (b * nq + i, 0)),
                  pl.BlockSpec((NSA_KV_HEADS, 1, NSA_REP * TQ, ncp), lambda b, i: (0, i, 0, 0)),
                  pl.BlockSpec(bias_near.shape, lambda b, i: (0, 0, 0, 0)),
                  pl.BlockSpec(c2s.shape, lambda b, i: (0, 0)),
                  pl.BlockSpec((NSA_REP, 1, LANES), lambda b, i: (0, 0, 0))],
        out_specs=pl.BlockSpec((NSA_REP, TQ, LANES), lambda b, i: (0, b * nq + i, 0)),
        out_shape=jax.ShapeDtypeStruct((NSA_REP, n, LANES), jnp.bfloat16),
        scratch_shapes=[pltpu.VMEM((NSA_REP, TQ, LANES), jnp.float32)],
        compiler_params=_params(("arbitrary", "arbitrary")),
        name="sparse_attention",
    )(q4, ks, vs, kw, vw, cmp_kv, gates, bias_cmp, bias_near, c2s, norm_g)


def _topk_rows(s, k):
    nrow = s.shape[0]
    rowf = lax.broadcasted_iota(jnp.int32, s.shape, 0).astype(jnp.float32)
    slot = lax.broadcasted_iota(jnp.int32, (k, s.shape[1]), 0)
    vals = jnp.zeros((k, s.shape[1]), jnp.float32)
    idxs = jnp.zeros((k, s.shape[1]), jnp.float32)
    for j in range(k):
        m = jnp.max(s, axis=0, keepdims=True)
        first = jnp.min(jnp.where(s == m, rowf, float(nrow)), axis=0, keepdims=True)
        vals = jnp.where(slot == j, m, vals)
        idxs = jnp.where(slot == j, first, idxs)
        s = jnp.where(rowf == first, -jnp.inf, s)
    return vals, idxs


def _route_kernel(attn_ref, conv_ref, x_ref, mod_ref, wout_ref, g_ref, wq_ref, keys_ref,
                  x1_ref, h2_ref, eid_ref, gate_ref):
    mixed = jnp.concatenate([attn_ref[r] for r in range(NSA_REP)] + [conv_ref[...]], axis=-1)
    x1 = x_ref[...] + mod_ref[0, 2:3, :] * _dot(mixed, wout_ref[...])
    x1_ref[...] = x1
    h2 = (_rms(x1, D_MODEL) * g_ref[...]) * (1.0 + mod_ref[0, 4:5, :]) + mod_ref[0, 3:4, :]
    h2_ref[...] = h2
    qp = _dot(h2.astype(jnp.bfloat16), wq_ref[...]).astype(jnp.bfloat16)

    tm = qp.shape[0]
    slot = lax.broadcasted_iota(jnp.int32, (PEER_TOPK, tm), 0)
    K = PEER_TOPK
    n_mid = K // 2 - 1
    row = lax.broadcasted_iota(jnp.int32, (K + SUBLANES * n_mid + SUBLANES, tm), 0)
    mid_a = (row - K) // SUBLANES + 1
    tail = row >= K + SUBLANES * n_mid
    cand_a = jnp.where(row < K, 0, jnp.where(tail, K // 2 + row % SUBLANES, mid_a))
    cand_b = jnp.where(row < K, row, jnp.where(tail, 0, row % SUBLANES))
    in_stair = (cand_a + 1) * (cand_b + 1) <= K
    rowf = (cand_a * K + cand_b).astype(jnp.float32)

    def staircase(first_half, second_half):
        parts = [first_half[0:1] + second_half]
        parts += [first_half[a:a + 1] + second_half[0:SUBLANES] for a in range(1, n_mid + 1)]
        parts += [first_half[K // 2:K] + second_half[0:1]]
        return jnp.concatenate(parts, axis=0)

    gates_t, rows_t = [], []
    for h in range(PEER_HEADS):
        blk = qp[:, h * PEER_QDIM:(h + 1) * PEER_QDIM]
        v1, i1 = _topk_rows(_dot_nt(keys_ref[2 * h], blk), PEER_TOPK)
        v2, i2 = _topk_rows(_dot_nt(keys_ref[2 * h + 1], blk), PEER_TOPK)
        cand = jnp.where(in_stair, staircase(v1, v2), -jnp.inf)
        eid = staircase(i1 * float(PEER_NKEYS), i2)
        top_s = jnp.zeros((PEER_TOPK, tm), jnp.float32)
        top_e = jnp.zeros((PEER_TOPK, tm), jnp.float32)
        for j in range(PEER_TOPK):
            m = jnp.max(cand, axis=0, keepdims=True)
            first = jnp.min(jnp.where(cand == m, rowf, 1e9), axis=0, keepdims=True)
            hit = rowf == first
            e = jnp.sum(jnp.where(hit, eid, 0.0), axis=0, keepdims=True)
            top_s = jnp.where(slot == j, m, top_s)
            top_e = jnp.where(slot == j, e, top_e)
            cand = jnp.where(hit, -jnp.inf, cand)
        ex = jnp.exp(top_s - jnp.max(top_s, axis=0, keepdims=True))
        gates_t.append(ex / jnp.sum(ex, axis=0, keepdims=True))
        rows_t.append(top_e * float(ROWS_PER_EXPERT))
    gate_ref[...] = jnp.concatenate(gates_t, axis=0).T
    eid_ref[...] = jnp.concatenate(rows_t, axis=0).T.astype(jnp.int32)


def _out_proj_route(attn4, conv_n, x2, mod3, w_out_p, ln_g, wq, keys_x, seq):
    n = x2.shape[0]
    tm = TM_OUT
    tps = seq // tm
    row = lambda i: (i, 0)
    const2 = lambda i: (0, 0)
    return pl.pallas_call(
        _route_kernel,
        grid=(n // tm,),
        in_specs=[pl.BlockSpec((NSA_REP, tm, LANES), lambda i: (0, i, 0)),
                  pl.BlockSpec((tm, CONV_WIDTH), row),
                  pl.BlockSpec((tm, D_MODEL), row),
                  pl.BlockSpec((1, 6, D_MODEL), lambda i: (i // tps, 0, 0)),
                  pl.BlockSpec((D_MODEL, D_MODEL), const2),
                  pl.BlockSpec((1, D_MODEL), const2),
                  pl.BlockSpec((D_MODEL, PEER_HEADS * PEER_QDIM), const2),
                  pl.BlockSpec((2 * PEER_HEADS, PEER_NKEYS, PEER_QDIM), lambda i: (0, 0, 0))],
        out_specs=[pl.BlockSpec((tm, D_MODEL), row), pl.BlockSpec((tm, D_MODEL), row),
                   pl.BlockSpec((tm, PEER_E), row), pl.BlockSpec((tm, PEER_E), row)],
        out_shape=[jax.ShapeDtypeStruct((n, D_MODEL), jnp.float32),
                   jax.ShapeDtypeStruct((n, D_MODEL), jnp.float32),
                   jax.ShapeDtypeStruct((n, PEER_E), jnp.int32),
                   jax.ShapeDtypeStruct((n, PEER_E), jnp.float32)],
        compiler_params=_params(("arbitrary",)),
        name="out_proj_route",
    )(attn4, conv_n, x2, mod3, w_out_p, ln_g, wq, keys_x)


ROWS_PER_EXPERT = ROW_CHUNKS // 2


IDX_TILE = TT * PEER_E
ROW_BUFFERS = 4
TOKENS_PER_TRIP = 16 * ROW_BUFFERS


def _index_copy(idx_hbm, idx_smem, sem, tile, slot):
    dst = idx_smem.at[pl.ds(pl.multiple_of(slot * IDX_TILE, IDX_TILE), IDX_TILE)]
    return pltpu.make_async_copy(idx_hbm.at[tile], dst, sem.at[slot])


def _fetch_indices(idx_hbm, idx_smem, sem):
    i = pl.program_id(0)
    slot = i % 2

    @pl.when(i == 0)
    def _():
        _index_copy(idx_hbm, idx_smem, sem, 0, 0).start()

    _index_copy(idx_hbm, idx_smem, sem, i, slot).wait()

    @pl.when(i + 1 < pl.num_programs(0))
    def _():
        _index_copy(idx_hbm, idx_smem, sem, i + 1, 1 - slot).start()

    return slot


def _gather_rows(slot, t, idx_smem, tab_ref, rows_ref):
    base = slot * IDX_TILE + t * PEER_E
    for k in range(PEER_E):
        if k % SUBLANES == 0:
            view = idx_smem.at[pl.ds(pl.multiple_of(base + k, SUBLANES), SUBLANES)]
        e = view[k % SUBLANES]
        row0 = pl.multiple_of(e, ROWS_PER_EXPERT)
        rows_ref[k * ROWS_PER_EXPERT:(k + 1) * ROWS_PER_EXPERT, :] = (
            tab_ref[pl.ds(row0, ROWS_PER_EXPERT), :])


def _token_stream(gather, compute, bufs):
    nb = len(bufs)
    gather(0, bufs[0])
    gather(1, bufs[1])

    def one(t, ring_pos):
        compute(t, bufs[ring_pos % nb])
        gather(jnp.minimum(t + 2, TT - 1), bufs[(ring_pos + 2) % nb])

    def trip(i, _):
        for j in range(TOKENS_PER_TRIP):
            one(TOKENS_PER_TRIP * i + j, j)
        return 0

    n_trips = TT // TOKENS_PER_TRIP
    lax.fori_loop(0, n_trips, trip, 0)
    for t in range(n_trips * TOKENS_PER_TRIP, TT):
        one(t, t)


def _split_bf16(x):
    hi = x.astype(jnp.bfloat16).astype(jnp.float32)
    return jnp.concatenate([hi, x - hi], axis=0).astype(jnp.bfloat16)


def _expert_of_row():
    n = lax.broadcasted_iota(jnp.int32, (PEER_E * ROW_CHUNKS, PEER_E), 0) // ROW_CHUNKS
    k = lax.broadcasted_iota(jnp.int32, (PEER_E * ROW_CHUNKS, PEER_E), 1)
    return jnp.where(n == k, 1.0, 0.0).astype(jnp.bfloat16)


def _peer_act_kernel(idx_hbm, x_ref, gate_ref, tab_ref, w_ref, idx_smem, sem, z_ref, *bufs):
    slot = _fetch_indices(idx_hbm, idx_smem, sem)
    m_i = lax.broadcasted_iota(jnp.int32, (2 * ROW_CHUNKS, PEER_E * ROW_CHUNKS), 0) % ROW_CHUNKS
    n_i = lax.broadcasted_iota(jnp.int32, (2 * ROW_CHUNKS, PEER_E * ROW_CHUNKS), 1) % ROW_CHUNKS
    chunk_match = m_i == n_i

    def compute(t, rows_ref):
        x_row = x_ref[pl.ds(t, 1), :]
        x8 = jnp.concatenate([x_row[:, c * LANES:(c + 1) * LANES] for c in range(ROW_CHUNKS)],
                             axis=0)
        x16 = _split_bf16(x8)
        u = pltpu.bitcast(rows_ref[...], jnp.bfloat16)
        r = _dot_nt(x16, u)
        z_ref[pl.ds(t, 1), :] = jnp.sum(jnp.where(chunk_match, r, 0.0), axis=0, keepdims=True)

    gather = lambda t, rows_ref: _gather_rows(slot, t, idx_smem, tab_ref, rows_ref)
    _token_stream(gather, compute, bufs)
    z_hi, z_lo = _hi_lo(z_ref[...])
    owner = _expert_of_row()
    act = _dot(z_hi, owner) + _dot(z_lo, owner)
    w_ref[...] = gate_ref[...] * jax.nn.gelu(act)


def _peer_out_kernel(idx_hbm, w_ref, tab_ref, x1_ref, mod_ref, g_ref, o_ref,
                     idx_smem, sem, w8_ref, *bufs):
    slot = _fetch_indices(idx_hbm, idx_smem, sem)
    w_hi, w_lo = _hi_lo(w_ref[...])
    owner = _expert_of_row()
    w8_ref[...] = _dot_nt(w_hi, owner) + _dot_nt(w_lo, owner)
    m_i = lax.broadcasted_iota(jnp.int32, (ROW_CHUNKS, PEER_E * ROW_CHUNKS), 0)
    n_i = lax.broadcasted_iota(jnp.int32, (ROW_CHUNKS, PEER_E * ROW_CHUNKS), 1) % ROW_CHUNKS
    chunk_match = m_i == n_i

    def compute(t, rows_ref):
        lhs = _split_bf16(jnp.where(chunk_match, w8_ref[pl.ds(t, 1), :], 0.0))
        v = pltpu.bitcast(rows_ref[...], jnp.bfloat16)
        o = _dot(lhs, v)
        o8 = o[0:ROW_CHUNKS] + o[ROW_CHUNKS:2 * ROW_CHUNKS]
        tops = [o8 if c == 0 else pltpu.roll(o8, ROW_CHUNKS - c, 0) for c in range(ROW_CHUNKS)]
        o_ref[pl.ds(t, 1), :] = jnp.concatenate([top[0:1, :] for top in tops], axis=1)

    gather = lambda t, rows_ref: _gather_rows(slot, t, idx_smem, tab_ref, rows_ref)
    _token_stream(gather, compute, bufs)
    x2 = x1_ref[...] + mod_ref[0, 5:6, :] * o_ref[...]
    o_ref[...] = _rms(x2, D_MODEL) * g_ref[...]


def _peer_scratch():
    rows = pltpu.VMEM((PEER_E * ROWS_PER_EXPERT, LANES), jnp.int32)
    return [pltpu.SMEM((2 * IDX_TILE,), jnp.int32), pltpu.SemaphoreType.DMA((2,)),
            pltpu.VMEM((TT, PEER_E * ROW_CHUNKS), jnp.float32)] + [rows] * ROW_BUFFERS


def _peer_act(idx_tiles, x8, gates, tab_u):
    n = x8.shape[0]
    return pl.pallas_call(
        _peer_act_kernel,
        grid=(n // TT,),
        in_specs=[pl.BlockSpec(memory_space=pl.ANY),
                  pl.BlockSpec((TT, D_MODEL), lambda i: (i, 0)),
                  pl.BlockSpec((TT, PEER_E), lambda i: (i, 0)),
                  pl.BlockSpec(tab_u.shape, lambda i: (0, 0))],
        out_specs=pl.BlockSpec((TT, PEER_E), lambda i: (i, 0)),
        out_shape=jax.ShapeDtypeStruct((n, PEER_E), jnp.float32),
        scratch_shapes=_peer_scratch(),
        compiler_params=_params(("arbitrary",)),
        name="peer_activations",
    )(idx_tiles, x8, gates, tab_u)


def _peer_out(idx_tiles, w, tab_v, x1, mod3, final_g, seq):
    n = w.shape[0]
    tps = seq // TT
    return pl.pallas_call(
        _peer_out_kernel,
        grid=(n // TT,),
        in_specs=[pl.BlockSpec(memory_space=pl.ANY),
                  pl.BlockSpec((TT, PEER_E), lambda i: (i, 0)),
                  pl.BlockSpec(tab_v.shape, lambda i: (0, 0)),
                  pl.BlockSpec((TT, D_MODEL), lambda i: (i, 0)),
                  pl.BlockSpec((1, 6, D_MODEL), lambda i: (i // tps, 0, 0)),
                  pl.BlockSpec((1, D_MODEL), lambda i: (0, 0))],
        out_specs=pl.BlockSpec((TT, D_MODEL), lambda i: (i, 0)),
        out_shape=jax.ShapeDtypeStruct((n, D_MODEL), jnp.float32),
        scratch_shapes=_peer_scratch(),
        compiler_params=_params(("arbitrary",)),
        name="peer_combine",
    )(idx_tiles, w, tab_v, x1, mod3, final_g)


def _pack_table(tab):
    bits = lax.bitcast_convert_type(tab.astype(jnp.bfloat16), jnp.uint16).astype(jnp.uint32)
    bits = bits.reshape(tab.shape[0] * ROWS_PER_EXPERT, 2 * LANES)
    words = bits[:, :LANES] | (bits[:, LANES:] << 16)
    return lax.bitcast_convert_type(words, jnp.int32)


def _attn_perm():
    idx = []
    for r in range(NSA_REP):
        for g in range(NSA_KV_HEADS):
            h = g * NSA_REP + r
            idx.extend(range(h * HEAD_DIM, (h + 1) * HEAD_DIM))
    return np.asarray(idx, np.int32)


def _rel_bucket_np(dist):
    n = np.maximum(dist, 0)
    exact = REL_BUCKETS // 2
    log_ratio = (np.log(np.maximum(n, 1).astype(np.float32) / np.float32(exact))
                 / np.float32(math.log(REL_MAX_DIST / exact)))
    large = exact + (log_ratio * np.float32(REL_BUCKETS - exact)).astype(np.int32)
    return np.where(n < exact, n, np.minimum(large, REL_BUCKETS - 1)).astype(np.int32)


def _bias_tables(rel_table, seq):
    nq = seq // TQ
    ncp = seq // CMP_STRIDE

    def lookup(bucket):
        onehot = (jnp.asarray(bucket)[..., None] == jnp.arange(REL_BUCKETS)).astype(jnp.float32)
        out = jnp.einsum("...b,bh->...h", onehot, rel_table, precision=HIGHEST)
        return out.reshape(bucket.shape + (NSA_KV_HEADS, NSA_REP))
    n_near = -(-REL_MAX_DIST // TQ) + 1
    r_i = np.arange(TQ)[:, None]
    c_i = np.arange(TK)[None, :]
    near = np.stack([_rel_bucket_np(d * TQ + r_i - c_i) for d in range(n_near)]
                    + [np.full((TQ, TK), REL_BUCKETS - 1, np.int32)])
    b_near = jnp.transpose(lookup(near), (3, 0, 4, 1, 2))
    b_near = b_near.reshape(NSA_KV_HEADS, n_near + 1, NSA_REP * TQ, TK)
    t_i = np.arange(seq)[:, None]
    n_i = np.arange(ncp)[None, :]
    cmp_b = _rel_bucket_np(t_i - (n_i * CMP_STRIDE + CMP_BLK - 1))
    b_cmp = jnp.transpose(lookup(cmp_b), (2, 3, 0, 1))
    b_cmp = b_cmp.reshape(NSA_KV_HEADS, NSA_REP, nq, TQ, ncp).transpose(0, 2, 1, 3, 4)
    b_cmp = b_cmp.reshape(NSA_KV_HEADS, nq, NSA_REP * TQ, ncp)
    return b_cmp, b_near


def _cmp_to_slc(seq):
    ncp = seq // CMP_STRIDE
    n_cmp = (seq - CMP_BLK) // CMP_STRIDE + 1
    n_slc = seq // SEL_BLK
    cs = np.arange(ncp) * CMP_STRIDE
    ss = np.arange(LANES) * SEL_BLK
    ov = (cs[:, None] < ss[None, :] + SEL_BLK) & (cs[:, None] + CMP_BLK > ss[None, :])
    ov &= (np.arange(ncp)[:, None] < n_cmp) & (np.arange(LANES)[None, :] < n_slc)
    return jnp.asarray(ov.T, jnp.bfloat16)


def kernel(x, c, ln_mix_g, ln_ffn_g, w_mod, b_mod, w_in, cmp_pe_k, cmp_pe_v, cmp_wk1, cmp_wk2,
           cmp_wv1, cmp_wv2, conv_w, norm_attn_g, norm_conv_g, w_out, peer_wq, peer_keys, peer_u,
           peer_v, rel_table, ln_final_g):
    B, S, _ = x.shape
    n = B * S
    assert w_mod.shape[0] == 1 and S % TM_PROJ == 0 and S % TQ == 0 and n % TT == 0
    assert TQ == TK and WINDOW % TK == 0
    x2 = x.reshape(n, D_MODEL)
    perm = _attn_perm()

    mod3 = _modulation(c, w_mod[0], b_mod[0]).reshape(B, 6, D_MODEL)

    w = w_in[0]
    q_cols = w[:, :NSA_WIDTH][:, perm]
    kv_cols = w[:, NSA_WIDTH:NSA_WIDTH + 6 * KV_W]
    g0 = NSA_WIDTH + 6 * KV_W
    gate_cols = jnp.pad(w[:, g0:g0 + NSA_HEADS * 3], ((0, 0), (0, LANES - NSA_HEADS * 3)))
    conv_cols = w[:, g0 + NSA_HEADS * 3:]
    w_perm = jnp.concatenate([q_cols, kv_cols, gate_cols, conv_cols], axis=1).astype(jnp.bfloat16)

    q4, kc, vc, ks, vs, kw, vw, gates, conv_n = _in_projection(
        x2, mod3, ln_mix_g[0].reshape(1, D_MODEL), w_perm, conv_w[0].reshape(CONV_K, CONV_WIDTH),
        norm_conv_g[0].reshape(1, CONV_WIDTH), S)

    ncp = S // CMP_STRIDE
    tok = jnp.stack([kc, vc]).reshape(2, B, ncp, CMP_STRIDE, NSA_KV_HEADS, HEAD_DIM)
    tok = tok.transpose(0, 1, 4, 2, 3, 5).reshape(2, B, NSA_KV_HEADS, ncp, CMP_STRIDE * HEAD_DIM)
    cmp_kv = _compress(tok,
                       jnp.stack([cmp_pe_k[0], cmp_pe_v[0]]).reshape(2, 1, CMP_BLK * HEAD_DIM),
                       jnp.stack([cmp_wk1[0], cmp_wv1[0]]), jnp.stack([cmp_wk2[0], cmp_wv2[0]]))

    b_cmp, b_near = _bias_tables(rel_table, S)
    attn4 = _attention(q4, ks, vs, kw, vw, cmp_kv, gates, b_cmp, b_near, _cmp_to_slc(S),
                       norm_attn_g[0][perm].reshape(NSA_REP, 1, LANES), B, S)

    w_out_p = jnp.concatenate([w_out[0][:NSA_WIDTH][perm], w_out[0][NSA_WIDTH:]], axis=0)
    half = PEER_QDIM // 2
    keys = peer_keys[0].reshape(2 * PEER_HEADS, PEER_NKEYS, half)
    lo = jnp.pad(keys, ((0, 0), (0, 0), (0, half)))
    hi = jnp.pad(keys, ((0, 0), (0, 0), (half, 0)))
    is_hi = (jnp.arange(2 * PEER_HEADS) % 2 == 1)[:, None, None]
    keys_x = jnp.where(is_hi, hi, lo).astype(jnp.bfloat16)
    x1, h2, idx_rows, peer_gates = _out_proj_route(
        attn4, conv_n, x2, mod3, w_out_p.astype(jnp.bfloat16), ln_ffn_g[0].reshape(1, D_MODEL),
        peer_wq[0].astype(jnp.bfloat16), keys_x, S)

    idx_tiles = idx_rows.reshape(n // TT, TT * PEER_E)
    w_act = _peer_act(idx_tiles, h2, peer_gates, _pack_table(peer_u[0]))
    out = _peer_out(idx_tiles, w_act, _pack_table(peer_v[0]), x1, mod3,
                    ln_final_g.reshape(1, D_MODEL), S)
    return out.reshape(B, S, D_MODEL)
```

```python
import functools
import math

import numpy as np
import jax
import jax.numpy as jnp
from jax import lax
from jax.experimental import pallas as pl
from jax.experimental.pallas import tpu as pltpu

D_MODEL = 1024
HEAD_DIM = 64
NSA_HEADS = 8
NSA_KV_HEADS = 2
NSA_REP = NSA_HEADS // NSA_KV_HEADS
NSA_WIDTH = NSA_HEADS * HEAD_DIM
KV_W = NSA_KV_HEADS * HEAD_DIM
CONV_WIDTH = D_MODEL - NSA_WIDTH
CONV_K = 3
CMP_BLK = 32
CMP_STRIDE = 16
SEL_BLK = 64
SEL_TOPN = 8
WINDOW = 512
FORCE_SCORE = 1e4
REL_BUCKETS = 32
REL_MAX_DIST = 128
PEER_HEADS = 8
PEER_NKEYS = 128
PEER_EXPERTS = PEER_NKEYS * PEER_NKEYS
PEER_QDIM = 128
PEER_TOPK = 16
PEER_E = PEER_HEADS * PEER_TOPK
EPS = 1e-6

LANES = 128
SUBLANES = 8
ROW_CHUNKS = D_MODEL // LANES
VMEM_LIMIT = 56 * 1024 * 1024

TM_PROJ = 512
TM_OUT = 256
TQ = 256
TK = 256
TT = 128
NEG = -1e30
HIGHEST = lax.Precision.HIGHEST

C_Q = 0
C_KV = C_Q + NSA_WIDTH
C_GATE = C_KV + 6 * KV_W
C_CONV = C_GATE + LANES
N_COLS = C_CONV + 3 * CONV_WIDTH


def _params(sem):
    return pltpu.CompilerParams(dimension_semantics=sem, vmem_limit_bytes=VMEM_LIMIT)


def _dot(a, b, precision=None):
    return jnp.dot(a, b, preferred_element_type=jnp.float32, precision=precision)


def _dot_nt(a, b, precision=None):
    return lax.dot_general(a, b, (((1,), (1,)), ((), ())),
                           preferred_element_type=jnp.float32, precision=precision)


def _rms(x, n):
    return x * lax.rsqrt(jnp.sum(x * x, axis=-1, keepdims=True) * (1.0 / n) + EPS)


def _mod_kernel(c_ref, w_ref, b_ref, o_ref):
    c = c_ref[...]
    act = c * jax.nn.sigmoid(c)
    o_ref[...] = _dot(act, w_ref[...], HIGHEST) + b_ref[...]


def _modulation(c, w_mod, b_mod):
    B = c.shape[0]
    n = w_mod.shape[1]
    bn = D_MODEL
    return pl.pallas_call(
        _mod_kernel,
        grid=(n // bn,),
        in_specs=[pl.BlockSpec((B, D_MODEL), lambda j: (0, 0)),
                  pl.BlockSpec((D_MODEL, bn), lambda j: (0, j)),
                  pl.BlockSpec((1, bn), lambda j: (0, j))],
        out_specs=pl.BlockSpec((B, bn), lambda j: (0, j)),
        out_shape=jax.ShapeDtypeStruct((B, n), jnp.float32),
        compiler_params=_params(("arbitrary",)),
        name="modulation",
    )(c, w_mod, b_mod.reshape(1, n))


def _inproj_kernel(tiles_per_seq, x_ref, mod_ref, g_ref, w_ref, cw_ref, cg_ref,
                   q_ref, kc_ref, vc_ref, ks_ref, vs_ref, kw_ref, vw_ref, gate_ref, conv_ref,
                   carry_ref):
    i = pl.program_id(0)
    x = x_ref[...]
    sh = mod_ref[0, 0:1, :]
    sc = mod_ref[0, 1:2, :]
    h = (_rms(x, D_MODEL) * g_ref[...]) * (1.0 + sc) + sh
    proj = _dot(h.astype(jnp.bfloat16), w_ref[...])

    for r in range(NSA_REP):
        q_ref[r] = (proj[:, C_Q + r * LANES:C_Q + (r + 1) * LANES] * (HEAD_DIM ** -0.5)
                    ).astype(jnp.bfloat16)
    kv = lambda k: proj[:, C_KV + k * KV_W:C_KV + (k + 1) * KV_W]
    kc_ref[...] = kv(0)
    vc_ref[...] = kv(1)
    ks_ref[...] = kv(2).astype(jnp.bfloat16)
    vs_ref[...] = kv(3).astype(jnp.bfloat16)
    kw_ref[...] = kv(4).astype(jnp.bfloat16)
    vw_ref[...] = kv(5).astype(jnp.bfloat16)
    gate_ref[...] = jax.nn.sigmoid(proj[:, C_GATE:C_GATE + LANES])

    cb = proj[:, C_CONV:C_CONV + CONV_WIDTH]
    cc = proj[:, C_CONV + CONV_WIDTH:C_CONV + 2 * CONV_WIDTH]
    ch = proj[:, C_CONV + 2 * CONV_WIDTH:C_CONV + 3 * CONV_WIDTH]
    z = cc * ch
    tm = z.shape[0]

    @pl.when(i % tiles_per_seq == 0)
    def _():
        carry_ref[...] = jnp.zeros_like(carry_ref)

    prev1 = carry_ref[SUBLANES - 1:SUBLANES, :]
    prev2 = carry_ref[SUBLANES - 2:SUBLANES - 1, :]
    row = lax.broadcasted_iota(jnp.int32, (tm, CONV_WIDTH), 0)
    z1 = jnp.where(row == 0, prev1, pltpu.roll(z, 1, 0))
    z2 = pltpu.roll(z, 2, 0)
    z2 = jnp.where(row == 0, prev2, jnp.where(row == 1, prev1, z2))
    carry_ref[...] = z[tm - SUBLANES:, :]
    y = cw_ref[0:1, :] * z2 + cw_ref[1:2, :] * z1 + cw_ref[2:3, :] * z
    conv_ref[...] = (_rms(cb * y, CONV_WIDTH) * cg_ref[...]).astype(jnp.bfloat16)


def _in_projection(x2, mod3, ln_g, w_perm, conv_w, conv_g, seq):
    n = x2.shape[0]
    tm = TM_PROJ
    tps = seq // tm
    row = lambda i: (i, 0)
    kv_f32 = jax.ShapeDtypeStruct((n, KV_W), jnp.float32)
    kv_bf = jax.ShapeDtypeStruct((n, KV_W), jnp.bfloat16)
    return pl.pallas_call(
        functools.partial(_inproj_kernel, tps),
        grid=(n // tm,),
        in_specs=[pl.BlockSpec((tm, D_MODEL), row),
                  pl.BlockSpec((1, 6, D_MODEL), lambda i: (i // tps, 0, 0)),
                  pl.BlockSpec((1, D_MODEL), lambda i: (0, 0)),
                  pl.BlockSpec((D_MODEL, N_COLS), lambda i: (0, 0)),
                  pl.BlockSpec((CONV_K, CONV_WIDTH), lambda i: (0, 0)),
                  pl.BlockSpec((1, CONV_WIDTH), lambda i: (0, 0))],
        out_specs=[pl.BlockSpec((NSA_REP, tm, LANES), lambda i: (0, i, 0))]
                  + [pl.BlockSpec((tm, KV_W), row)] * 6
                  + [pl.BlockSpec((tm, LANES), row), pl.BlockSpec((tm, CONV_WIDTH), row)],
        out_shape=[jax.ShapeDtypeStruct((NSA_REP, n, LANES), jnp.bfloat16),
                   kv_f32, kv_f32, kv_bf, kv_bf, kv_bf, kv_bf,
                   jax.ShapeDtypeStruct((n, LANES), jnp.float32),
                   jax.ShapeDtypeStruct((n, CONV_WIDTH), jnp.bfloat16)],
        scratch_shapes=[pltpu.VMEM((SUBLANES, CONV_WIDTH), jnp.float32)],
        compiler_params=_params(("arbitrary",)),
        name="in_projection",
    )(x2, mod3, ln_g, w_perm, conv_w, conv_g)


def _compress_kernel(tok_ref, pe_ref, w1_ref, w2_ref, o_ref):
    half = (CMP_BLK // 2) * HEAD_DIM
    w1a = w1_ref[0, 0:half, :]
    w1b = w1_ref[0, half:2 * half, :]
    pe_term = _dot(pe_ref[0], w1_ref[0], HIGHEST)
    ncp = tok_ref.shape[3]
    row = lax.broadcasted_iota(jnp.int32, (ncp, HEAD_DIM), 0)
    for g in range(NSA_KV_HEADS):
        t = tok_ref[0, 0, g]
        a = _dot(t, w1a, HIGHEST)
        b = _dot(t, w1b, HIGHEST)
        pre = a + pltpu.roll(b, ncp - 1, 0) + pe_term
        out = _dot(jax.nn.gelu(pre), w2_ref[0], HIGHEST)
        o_ref[0, 0, :, g * HEAD_DIM:(g + 1) * HEAD_DIM] = jnp.where(row < ncp - 1, out, 0.0)


def _compress(tok, pe, w1, w2):
    _, B, G, ncp, width = tok.shape
    return pl.pallas_call(
        _compress_kernel,
        grid=(2, B),
        in_specs=[pl.BlockSpec((1, 1, G, ncp, width), lambda k, b: (k, b, 0, 0, 0)),
                  pl.BlockSpec((1, 1, CMP_BLK * HEAD_DIM), lambda k, b: (k, 0, 0)),
                  pl.BlockSpec((1, CMP_BLK * HEAD_DIM, HEAD_DIM), lambda k, b: (k, 0, 0)),
                  pl.BlockSpec((1, HEAD_DIM, HEAD_DIM), lambda k, b: (k, 0, 0))],
        out_specs=pl.BlockSpec((1, 1, ncp, KV_W), lambda k, b: (k, b, 0, 0)),
        out_shape=jax.ShapeDtypeStruct((2, B, ncp, KV_W), jnp.float32),
        compiler_params=_params(("arbitrary", "arbitrary")),
        name="compress_kv",
    )(tok, pe, w1, w2)


M_FLOOR = -1e29


def _hi_lo(x):
    hi = x.astype(jnp.bfloat16)
    return hi, (x - hi.astype(jnp.float32)).astype(jnp.bfloat16)


def _attn_kernel(n_slc, q_ref, ks_ref, vs_ref, kw_ref, vw_ref, cmp_ref, gate_ref,
                 bcmp_ref, bnear_ref, c2s_ref, ng_ref, o_ref, og0_ref):
    qi = pl.program_id(1)
    t0 = qi * TQ
    rows = NSA_REP * TQ
    ncp = cmp_ref.shape[2]
    q4 = q_ref[...].reshape(rows, LANES)
    lane = lax.broadcasted_iota(jnp.int32, (1, LANES), 1)
    tq_col = t0 + lax.broadcasted_iota(jnp.int32, (TQ, 1), 0)
    t_col = jnp.concatenate([tq_col] * NSA_REP, axis=0)
    rowq = lax.broadcasted_iota(jnp.int32, (TQ, 1), 0)
    colk = lax.broadcasted_iota(jnp.int32, (1, TK), 1)
    gates_hi, gates_lo = _hi_lo(gate_ref[...])
    n_near = bnear_ref.shape[1] - 1

    groups = []
    for g in range(NSA_KV_HEADS):
        in_g = (lane // HEAD_DIM) == g
        q = jnp.where(in_g, q4, jnp.zeros_like(q4))

        kc_hi, kc_lo = _hi_lo(cmp_ref[0, 0])
        s = _dot_nt(q, kc_hi) + _dot_nt(q, kc_lo) + bcmp_ref[g, 0]
        n_idx = lax.broadcasted_iota(jnp.int32, (1, ncp), 1)
        valid = (n_idx * CMP_STRIDE + (CMP_BLK - 1) <= t_col) & (n_idx < ncp - 1)
        s = jnp.where(valid, s, NEG)
        e = jnp.where(valid, jnp.exp(s - jnp.max(s, axis=-1, keepdims=True)), 0.0)
        e_hi, e_lo = _hi_lo(e)
        ones = jnp.ones((ncp, ncp), jnp.bfloat16)
        den = _dot(e_hi, ones) + _dot(e_lo, ones)
        p_cmp = e / jnp.maximum(den, 1e-30)
        o_cmp = _dot(p_cmp.astype(jnp.bfloat16), cmp_ref[1, 0].astype(jnp.bfloat16))

        p_sum = p_cmp[0:TQ]
        for r in range(1, NSA_REP):
            p_sum = p_sum + p_cmp[r * TQ:(r + 1) * TQ]
        ps_hi, ps_lo = _hi_lo(p_sum)
        imp = (_dot_nt(c2s_ref[...], ps_hi) + _dot_nt(c2s_ref[...], ps_lo))[0:n_slc]
        blk = lax.broadcasted_iota(jnp.int32, (n_slc, TQ), 0)
        blk_f = blk.astype(jnp.float32)
        tok = t0 + lax.broadcasted_iota(jnp.int32, (1, TQ), 1)
        cur = tok // SEL_BLK
        forced = (blk == 0) | (blk == cur) | (blk == cur - 1)
        score = jnp.where(forced, FORCE_SCORE, jnp.where(blk * SEL_BLK <= tok, imp, -jnp.inf))
        taken = jnp.zeros((n_slc, TQ), jnp.bool_)
        for _ in range(min(SEL_TOPN, n_slc)):
            eff = jnp.where(taken, -jnp.inf, score)
            mx = jnp.max(eff, axis=0, keepdims=True)
            cand = jnp.where((eff == mx) & jnp.logical_not(taken), blk_f, float(LANES))
            taken = taken | (blk_f == jnp.min(cand, axis=0, keepdims=True))
        sel_t = jnp.concatenate([jnp.where(taken, 1.0, 0.0),
                                 jnp.zeros((LANES - n_slc, TQ), jnp.float32)], axis=0)
        sel = sel_t.T.astype(jnp.bfloat16)
        groups.append((in_g, q, o_cmp, sel))

    blk_row = lax.broadcasted_iota(jnp.int32, (LANES, TK), 0)
    blk_col = lax.broadcasted_iota(jnp.int32, (LANES, TK), 1) // SEL_BLK
    causal_add = jnp.where(rowq - colk >= 0, 0.0, NEG)
    edge_add = jnp.where(rowq - colk < 0, 0.0, NEG)

    def step(g, k, v, kt, mask_add, carry):
        in_g, q = groups[g][0], groups[g][1]
        m_old, acc_old = carry
        s = _dot_nt(q, k) + bnear_ref[g, jnp.minimum(qi - kt, n_near)]
        s = (s.reshape(NSA_REP, TQ, TK) + mask_add[None]).reshape(rows, TK)
        m_new = jnp.maximum(m_old, jnp.max(s, axis=-1, keepdims=True))
        p = jnp.exp(s - m_new).astype(jnp.bfloat16)
        v_aug = jnp.where(in_g, v, jnp.ones((TK, LANES), jnp.bfloat16))
        return m_new, jnp.exp(m_old - m_new) * acc_old + _dot(p, v_aug)

    def slc_step(kt, carry):
        at = pl.ds(pl.multiple_of(kt * TK, TK), TK)
        k, v = ks_ref[at, :], vs_ref[at, :]
        expand = jnp.where(blk_row == blk_col + kt * (TK // SEL_BLK), 1.0, 0.0).astype(jnp.bfloat16)
        diag = jnp.where(kt == qi, causal_add, 0.0)
        out = []
        for g in range(NSA_KV_HEADS):
            chosen = _dot(groups[g][3], expand)
            out.append(step(g, k, v, kt, (chosen - 1.0) * (-NEG) + diag, carry[g]))
        return tuple(out)

    def win_step(kt, carry):
        at = pl.ds(pl.multiple_of(kt * TK, TK), TK)
        k, v = kw_ref[at, :], vw_ref[at, :]
        mask_add = jnp.where(kt == qi, causal_add,
                             jnp.where(kt == qi - WINDOW // TK, edge_add, 0.0))
        return tuple(step(g, k, v, kt, mask_add, carry[g]) for g in range(NSA_KV_HEADS))

    def finish(acc):
        return acc / jnp.maximum(pltpu.roll(acc, HEAD_DIM, 1), 1e-30)

    init = ((jnp.full((rows, 1), M_FLOOR, jnp.float32), jnp.zeros((rows, LANES), jnp.float32)),
            ) * NSA_KV_HEADS
    slc = lax.fori_loop(0, qi + 1, slc_step, init)
    win = lax.fori_loop(jnp.maximum(qi - WINDOW // TK, 0), qi + 1, win_step, init)

    for g in range(NSA_KV_HEADS):
        in_g, _, o_cmp, _ = groups[g]
        o_slc = finish(slc[g][1])
        o_win = finish(win[g][1])

        n_gate = NSA_REP * 3
        src = lax.broadcasted_iota(jnp.int32, (LANES, n_gate * LANES), 0)
        dst = lax.broadcasted_iota(jnp.int32, (LANES, n_gate * LANES), 1) // LANES
        spread = jnp.where(src == g * n_gate + dst, 1.0, 0.0).astype(jnp.bfloat16)
        gate_x = _dot(gates_hi, spread) + _dot(gates_lo, spread)
        for r in range(NSA_REP):
            sl = slice(r * TQ, (r + 1) * TQ)
            gate = lambda j: gate_x[:, (r * 3 + j) * LANES:(r * 3 + j + 1) * LANES]
            o = gate(0) * o_cmp[sl] + gate(1) * o_slc[sl] + gate(2) * o_win[sl]
            if g == 0:
                og0_ref[r] = o
            else:
                og0_ref[r] = jnp.where(in_g, o, og0_ref[r])

    ss = jnp.zeros((TQ, 1), jnp.float32)
    for r in range(NSA_REP):
        o = og0_ref[r]
        ss = ss + jnp.sum(o * o, axis=-1, keepdims=True)
    inv = lax.rsqrt(ss * (1.0 / NSA_WIDTH) + EPS)
    for r in range(NSA_REP):
        o_ref[r] = (og0_ref[r] * inv * ng_ref[r]).astype(jnp.bfloat16)


def _attention(q4, ks, vs, kw, vw, cmp_kv, gates, bias_cmp, bias_near, c2s, norm_g, batch, seq):
    n = batch * seq
    nq = seq // TQ
    ncp = cmp_kv.shape[2]
    n_slc = seq // SEL_BLK
    kv_spec = pl.BlockSpec((seq, KV_W), lambda b, i: (b, 0))
    return pl.pallas_call(
        functools.partial(_attn_kernel, n_slc),
        grid=(batch, nq),
        in_specs=[pl.BlockSpec((NSA_REP, TQ, LANES), lambda b, i: (0, b * nq + i, 0)),
                  kv_spec, kv_spec, kv_spec, kv_spec,
                  pl.BlockSpec((2, 1, ncp, KV_W), lambda b, i: (0, b, 0, 0)),
                  pl.BlockSpec((TQ, LANES), lambda b, i: (b * nq + i, 0)),
                  pl.BlockSpec((NSA_KV_HEADS, 1, NSA_REP * TQ, ncp), lambda b, i: (0, i, 0, 0)),
                  pl.BlockSpec(bias_near.shape, lambda b, i: (0, 0, 0, 0)),
                  pl.BlockSpec(c2s.shape, lambda b, i: (0, 0)),
                  pl.BlockSpec((NSA_REP, 1, LANES), lambda b, i: (0, 0, 0))],
        out_specs=pl.BlockSpec((NSA_REP, TQ, LANES), lambda b, i: (0, b * nq + i, 0)),
        out_shape=jax.ShapeDtypeStruct((NSA_REP, n, LANES), jnp.bfloat16),
        scratch_shapes=[pltpu.VMEM((NSA_REP, TQ, LANES), jnp.float32)],
        compiler_params=_params(("arbitrary", "arbitrary")),
        name="sparse_attention",
    )(q4, ks, vs, kw, vw, cmp_kv, gates, bias_cmp, bias_near, c2s, norm_g)


def _topk_rows(s, k):
    nrow, width = s.shape
    slabs = [s[v * SUBLANES:(v + 1) * SUBLANES] for v in range(nrow // SUBLANES)]
    sub = lax.broadcasted_iota(jnp.int32, (SUBLANES, width), 0).astype(jnp.float32)
    slot = lax.broadcasted_iota(jnp.int32, (k, width), 0)
    vals = jnp.zeros((k, width), jnp.float32)
    idxs = jnp.zeros((k, width), jnp.float32)
    for j in range(k):
        best, slab = slabs[0], jnp.zeros((SUBLANES, width), jnp.float32)
        for v in range(1, len(slabs)):
            upd = slabs[v] > best
            best = jnp.where(upd, slabs[v], best)
            slab = jnp.where(upd, float(v), slab)
        m = jnp.max(best, axis=0, keepdims=True)
        first = jnp.min(jnp.where(best == m, slab * SUBLANES + sub, float(nrow)), axis=0,
                        keepdims=True)
        vals = jnp.where(slot == j, m, vals)
        idxs = jnp.where(slot == j, first, idxs)
        off = first - sub
        slabs = [jnp.where(off == float(v * SUBLANES), -jnp.inf, sl) for v, sl in enumerate(slabs)]
    return vals, idxs


def _route_kernel(attn_ref, conv_ref, x_ref, mod_ref, wout_ref, g_ref, wq_ref, keys_ref,
                  x1_ref, h2_ref, eid_ref, gate_ref):
    mixed = jnp.concatenate([attn_ref[r] for r in range(NSA_REP)] + [conv_ref[...]], axis=-1)
    x1 = x_ref[...] + mod_ref[0, 2:3, :] * _dot(mixed, wout_ref[...])
    x1_ref[...] = x1
    h2 = (_rms(x1, D_MODEL) * g_ref[...]) * (1.0 + mod_ref[0, 4:5, :]) + mod_ref[0, 3:4, :]
    h2_ref[...] = h2
    qp = _dot(h2.astype(jnp.bfloat16), wq_ref[...]).astype(jnp.bfloat16)

    tm = qp.shape[0]
    slot = lax.broadcasted_iota(jnp.int32, (PEER_TOPK, tm), 0)
    K = PEER_TOPK
    n_mid = K // 2 - 1
    row = lax.broadcasted_iota(jnp.int32, (K + SUBLANES * n_mid + SUBLANES, tm), 0)
    mid_a = (row - K) // SUBLANES + 1
    tail = row >= K + SUBLANES * n_mid
    cand_a = jnp.where(row < K, 0, jnp.where(tail, K // 2 + row % SUBLANES, mid_a))
    cand_b = jnp.where(row < K, row, jnp.where(tail, 0, row % SUBLANES))
    in_stair = (cand_a + 1) * (cand_b + 1) <= K
    rowf = (cand_a * K + cand_b).astype(jnp.float32)

    def staircase(first_half, second_half):
        parts = [first_half[0:1] + second_half]
        parts += [first_half[a:a + 1] + second_half[0:SUBLANES] for a in range(1, n_mid + 1)]
        parts += [first_half[K // 2:K] + second_half[0:1]]
        return jnp.concatenate(parts, axis=0)

    gates_t, rows_t = [], []
    for h in range(PEER_HEADS):
        blk = qp[:, h * PEER_QDIM:(h + 1) * PEER_QDIM]
        v1, i1 = _topk_rows(_dot_nt(keys_ref[2 * h], blk), PEER_TOPK)
        v2, i2 = _topk_rows(_dot_nt(keys_ref[2 * h + 1], blk), PEER_TOPK)
        cand = jnp.where(in_stair, staircase(v1, v2), -jnp.inf)
        eid = staircase(i1 * float(PEER_NKEYS), i2)
        top_s = jnp.zeros((PEER_TOPK, tm), jnp.float32)
        top_e = jnp.zeros((PEER_TOPK, tm), jnp.float32)
        for j in range(PEER_TOPK):
            m = jnp.max(cand, axis=0, keepdims=True)
            first = jnp.min(jnp.where(cand == m, rowf, 1e9), axis=0, keepdims=True)
            hit = rowf == first
            e = jnp.sum(jnp.where(hit, eid, 0.0), axis=0, keepdims=True)
            top_s = jnp.where(slot == j, m, top_s)
            top_e = jnp.where(slot == j, e, top_e)
            cand = jnp.where(hit, -jnp.inf, cand)
        ex = jnp.exp(top_s - jnp.max(top_s, axis=0, keepdims=True))
        gates_t.append(ex / jnp.sum(ex, axis=0, keepdims=True))
        rows_t.append(top_e * float(ROWS_PER_EXPERT))
    gate_ref[...] = jnp.concatenate(gates_t, axis=0).T
    eid_ref[...] = jnp.concatenate(rows_t, axis=0).T.astype(jnp.int32)


def _out_proj_route(attn4, conv_n, x2, mod3, w_out_p, ln_g, wq, keys_x, seq):
    n = x2.shape[0]
    tm = TM_OUT
    tps = seq // tm
    row = lambda i: (i, 0)
    const2 = lambda i: (0, 0)
    return pl.pallas_call(
        _route_kernel,
        grid=(n // tm,),
        in_specs=[pl.BlockSpec((NSA_REP, tm, LANES), lambda i: (0, i, 0)),
                  pl.BlockSpec((tm, CONV_WIDTH), row),
                  pl.BlockSpec((tm, D_MODEL), row),
                  pl.BlockSpec((1, 6, D_MODEL), lambda i: (i // tps, 0, 0)),
                  pl.BlockSpec((D_MODEL, D_MODEL), const2),
                  pl.BlockSpec((1, D_MODEL), const2),
                  pl.BlockSpec((D_MODEL, PEER_HEADS * PEER_QDIM), const2),
                  pl.BlockSpec((2 * PEER_HEADS, PEER_NKEYS, PEER_QDIM), lambda i: (0, 0, 0))],
        out_specs=[pl.BlockSpec((tm, D_MODEL), row), pl.BlockSpec((tm, D_MODEL), row),
                   pl.BlockSpec((tm, PEER_E), row), pl.BlockSpec((tm, PEER_E), row)],
        out_shape=[jax.ShapeDtypeStruct((n, D_MODEL), jnp.float32),
                   jax.ShapeDtypeStruct((n, D_MODEL), jnp.float32),
                   jax.ShapeDtypeStruct((n, PEER_E), jnp.int32),
                   jax.ShapeDtypeStruct((n, PEER_E), jnp.float32)],
        compiler_params=_params(("arbitrary",)),
        name="out_proj_route",
    )(attn4, conv_n, x2, mod3, w_out_p, ln_g, wq, keys_x)


ROWS_PER_EXPERT = ROW_CHUNKS // 2


IDX_TILE = TT * PEER_E
ROW_BUFFERS = 4
TOKENS_PER_TRIP = 32 * ROW_BUFFERS


def _index_copy(idx_hbm, idx_smem, sem, tile, slot):
    dst = idx_smem.at[pl.ds(pl.multiple_of(slot * IDX_TILE, IDX_TILE), IDX_TILE)]
    return pltpu.make_async_copy(idx_hbm.at[tile], dst, sem.at[slot])


def _fetch_indices(idx_hbm, idx_smem, sem):
    i = pl.program_id(0)
    slot = i % 2

    @pl.when(i == 0)
    def _():
        _index_copy(idx_hbm, idx_smem, sem, 0, 0).start()

    _index_copy(idx_hbm, idx_smem, sem, i, slot).wait()

    @pl.when(i + 1 < pl.num_programs(0))
    def _():
        _index_copy(idx_hbm, idx_smem, sem, i + 1, 1 - slot).start()

    return slot


def _gather_rows(slot, t, idx_smem, tab_ref, rows_ref):
    base = slot * IDX_TILE + t * PEER_E
    for k in range(PEER_E):
        if k % SUBLANES == 0:
            view = idx_smem.at[pl.ds(pl.multiple_of(base + k, SUBLANES), SUBLANES)]
        e = view[k % SUBLANES]
        row0 = pl.multiple_of(e, ROWS_PER_EXPERT)
        rows_ref[k * ROWS_PER_EXPERT:(k + 1) * ROWS_PER_EXPERT, :] = (
            tab_ref[pl.ds(row0, ROWS_PER_EXPERT), :])


def _token_stream(gather, compute, bufs):
    nb = len(bufs)
    gather(0, bufs[0])
    gather(1, bufs[1])

    def one(t, ring_pos):
        compute(t, bufs[ring_pos % nb])
        gather(jnp.minimum(t + 2, TT - 1), bufs[(ring_pos + 2) % nb])

    def trip(i, _):
        for j in range(TOKENS_PER_TRIP):
            one(TOKENS_PER_TRIP * i + j, j)
        return 0

    n_trips = TT // TOKENS_PER_TRIP
    lax.fori_loop(0, n_trips, trip, 0)
    for t in range(n_trips * TOKENS_PER_TRIP, TT):
        one(t, t)


def _split_bf16(x):
    hi = x.astype(jnp.bfloat16).astype(jnp.float32)
    return jnp.concatenate([hi, x - hi], axis=0).astype(jnp.bfloat16)


def _expert_of_row():
    n = lax.broadcasted_iota(jnp.int32, (PEER_E * ROW_CHUNKS, PEER_E), 0) // ROW_CHUNKS
    k = lax.broadcasted_iota(jnp.int32, (PEER_E * ROW_CHUNKS, PEER_E), 1)
    return jnp.where(n == k, 1.0, 0.0).astype(jnp.bfloat16)


def _peer_act_kernel(idx_hbm, x_ref, gate_ref, tab_ref, w_ref, idx_smem, sem, z_ref, *bufs):
    slot = _fetch_indices(idx_hbm, idx_smem, sem)
    m_i = lax.broadcasted_iota(jnp.int32, (2 * ROW_CHUNKS, PEER_E * ROW_CHUNKS), 0) % ROW_CHUNKS
    n_i = lax.broadcasted_iota(jnp.int32, (2 * ROW_CHUNKS, PEER_E * ROW_CHUNKS), 1) % ROW_CHUNKS
    chunk_match = m_i == n_i

    def compute(t, rows_ref):
        x_row = x_ref[pl.ds(t, 1), :]
        x8 = jnp.concatenate([x_row[:, c * LANES:(c + 1) * LANES] for c in range(ROW_CHUNKS)],
                             axis=0)
        x16 = _split_bf16(x8)
        u = pltpu.bitcast(rows_ref[...], jnp.bfloat16)
        r = _dot_nt(x16, u)
        z_ref[pl.ds(t, 1), :] = jnp.sum(jnp.where(chunk_match, r, 0.0), axis=0, keepdims=True)

    gather = lambda t, rows_ref: _gather_rows(slot, t, idx_smem, tab_ref, rows_ref)
    _token_stream(gather, compute, bufs)
    z_hi, z_lo = _hi_lo(z_ref[...])
    owner = _expert_of_row()
    act = _dot(z_hi, owner) + _dot(z_lo, owner)
    w_ref[...] = gate_ref[...] * jax.nn.gelu(act)


def _peer_out_kernel(idx_hbm, w_ref, tab_ref, x1_ref, mod_ref, g_ref, o_ref,
                     idx_smem, sem, w8_ref, *bufs):
    slot = _fetch_indices(idx_hbm, idx_smem, sem)
    w_hi, w_lo = _hi_lo(w_ref[...])
    owner = _expert_of_row()
    w8_ref[...] = _dot_nt(w_hi, owner) + _dot_nt(w_lo, owner)
    m_i = lax.broadcasted_iota(jnp.int32, (ROW_CHUNKS, PEER_E * ROW_CHUNKS), 0)
    n_i = lax.broadcasted_iota(jnp.int32, (ROW_CHUNKS, PEER_E * ROW_CHUNKS), 1) % ROW_CHUNKS
    chunk_match = m_i == n_i

    def compute(t, rows_ref):
        lhs = _split_bf16(jnp.where(chunk_match, w8_ref[pl.ds(t, 1), :], 0.0))
        v = pltpu.bitcast(rows_ref[...], jnp.bfloat16)
        o = _dot(lhs, v)
        o8 = o[0:ROW_CHUNKS] + o[ROW_CHUNKS:2 * ROW_CHUNKS]
        tops = [o8 if c == 0 else pltpu.roll(o8, ROW_CHUNKS - c, 0) for c in range(ROW_CHUNKS)]
        o_ref[pl.ds(t, 1), :] = jnp.concatenate([top[0:1, :] for top in tops], axis=1)

    gather = lambda t, rows_ref: _gather_rows(slot, t, idx_smem, tab_ref, rows_ref)
    _token_stream(gather, compute, bufs)
    x2 = x1_ref[...] + mod_ref[0, 5:6, :] * o_ref[...]
    o_ref[...] = _rms(x2, D_MODEL) * g_ref[...]


def _peer_scratch():
    rows = pltpu.VMEM((PEER_E * ROWS_PER_EXPERT, LANES), jnp.int32)
    return [pltpu.SMEM((2 * IDX_TILE,), jnp.int32), pltpu.SemaphoreType.DMA((2,)),
            pltpu.VMEM((TT, PEER_E * ROW_CHUNKS), jnp.float32)] + [rows] * ROW_BUFFERS


def _peer_act(idx_tiles, x8, gates, tab_u):
    n = x8.shape[0]
    return pl.pallas_call(
        _peer_act_kernel,
        grid=(n // TT,),
        in_specs=[pl.BlockSpec(memory_space=pl.ANY),
                  pl.BlockSpec((TT, D_MODEL), lambda i: (i, 0)),
                  pl.BlockSpec((TT, PEER_E), lambda i: (i, 0)),
                  pl.BlockSpec(tab_u.shape, lambda i: (0, 0))],
        out_specs=pl.BlockSpec((TT, PEER_E), lambda i: (i, 0)),
        out_shape=jax.ShapeDtypeStruct((n, PEER_E), jnp.float32),
        scratch_shapes=_peer_scratch(),
        compiler_params=_params(("arbitrary",)),
        name="peer_activations",
    )(idx_tiles, x8, gates, tab_u)


def _peer_out(idx_tiles, w, tab_v, x1, mod3, final_g, seq):
    n = w.shape[0]
    tps = seq // TT
    return pl.pallas_call(
        _peer_out_kernel,
        grid=(n // TT,),
        in_specs=[pl.BlockSpec(memory_space=pl.ANY),
                  pl.BlockSpec((TT, PEER_E), lambda i: (i, 0)),
                  pl.BlockSpec(tab_v.shape, lambda i: (0, 0)),
                  pl.BlockSpec((TT, D_MODEL), lambda i: (i, 0)),
                  pl.BlockSpec((1, 6, D_MODEL), lambda i: (i // tps, 0, 0)),
                  pl.BlockSpec((1, D_MODEL), lambda i: (0, 0))],
        out_specs=pl.BlockSpec((TT, D_MODEL), lambda i: (i, 0)),
        out_shape=jax.ShapeDtypeStruct((n, D_MODEL), jnp.float32),
        scratch_shapes=_peer_scratch(),
        compiler_params=_params(("arbitrary",)),
        name="peer_combine",
    )(idx_tiles, w, tab_v, x1, mod3, final_g)


def _pack_table(tab):
    bits = lax.bitcast_convert_type(tab.astype(jnp.bfloat16), jnp.uint16).astype(jnp.uint32)
    bits = bits.reshape(tab.shape[0] * ROWS_PER_EXPERT, 2 * LANES)
    words = bits[:, :LANES] | (bits[:, LANES:] << 16)
    return lax.bitcast_convert_type(words, jnp.int32)


def _attn_perm():
    idx = []
    for r in range(NSA_REP):
        for g in range(NSA_KV_HEADS):
            h = g * NSA_REP + r
            idx.extend(range(h * HEAD_DIM, (h + 1) * HEAD_DIM))
    return np.asarray(idx, np.int32)


def _rel_bucket_np(dist):
    n = np.maximum(dist, 0)
    exact = REL_BUCKETS // 2
    log_ratio = (np.log(np.maximum(n, 1).astype(np.float32) / np.float32(exact))
                 / np.float32(math.log(REL_MAX_DIST / exact)))
    large = exact + (log_ratio * np.float32(REL_BUCKETS - exact)).astype(np.int32)
    return np.where(n < exact, n, np.minimum(large, REL_BUCKETS - 1)).astype(np.int32)


def _bias_tables(rel_table, seq):
    nq = seq // TQ
    ncp = seq // CMP_STRIDE

    def lookup(bucket):
        onehot = (jnp.asarray(bucket)[..., None] == jnp.arange(REL_BUCKETS)).astype(jnp.float32)
        out = jnp.einsum("...b,bh->...h", onehot, rel_table, precision=HIGHEST)
        return out.reshape(bucket.shape + (NSA_KV_HEADS, NSA_REP))
    n_near = -(-REL_MAX_DIST // TQ) + 1
    r_i = np.arange(TQ)[:, None]
    c_i = np.arange(TK)[None, :]
    near = np.stack([_rel_bucket_np(d * TQ + r_i - c_i) for d in range(n_near)]
                    + [np.full((TQ, TK), REL_BUCKETS - 1, np.int32)])
    b_near = jnp.transpose(lookup(near), (3, 0, 4, 1, 2))
    b_near = b_near.reshape(NSA_KV_HEADS, n_near + 1, NSA_REP * TQ, TK)
    t_i = np.arange(seq)[:, None]
    n_i = np.arange(ncp)[None, :]
    cmp_b = _rel_bucket_np(t_i - (n_i * CMP_STRIDE + CMP_BLK - 1))
    b_cmp = jnp.transpose(lookup(cmp_b), (2, 3, 0, 1))
    b_cmp = b_cmp.reshape(NSA_KV_HEADS, NSA_REP, nq, TQ, ncp).transpose(0, 2, 1, 3, 4)
    b_cmp = b_cmp.reshape(NSA_KV_HEADS, nq, NSA_REP * TQ, ncp)
    return b_cmp, b_near


def _cmp_to_slc(seq):
    ncp = seq // CMP_STRIDE
    n_cmp = (seq - CMP_BLK) // CMP_STRIDE + 1
    n_slc = seq // SEL_BLK
    cs = np.arange(ncp) * CMP_STRIDE
    ss = np.arange(LANES) * SEL_BLK
    ov = (cs[:, None] < ss[None, :] + SEL_BLK) & (cs[:, None] + CMP_BLK > ss[None, :])
    ov &= (np.arange(ncp)[:, None] < n_cmp) & (np.arange(LANES)[None, :] < n_slc)
    return jnp.asarray(ov.T, jnp.bfloat16)


def kernel(x, c, ln_mix_g, ln_ffn_g, w_mod, b_mod, w_in, cmp_pe_k, cmp_pe_v, cmp_wk1, cmp_wk2,
           cmp_wv1, cmp_wv2, conv_w, norm_attn_g, norm_conv_g, w_out, peer_wq, peer_keys, peer_u,
           peer_v, rel_table, ln_final_g):
    B, S, _ = x.shape
    n = B * S
    assert w_mod.shape[0] == 1 and S % TM_PROJ == 0 and S % TQ == 0 and n % TT == 0
    assert TQ == TK and WINDOW % TK == 0
    x2 = x.reshape(n, D_MODEL)
    perm = _attn_perm()

    mod3 = _modulation(c, w_mod[0], b_mod[0]).reshape(B, 6, D_MODEL)

    w = w_in[0]
    q_cols = w[:, :NSA_WIDTH][:, perm]
    kv_cols = w[:, NSA_WIDTH:NSA_WIDTH + 6 * KV_W]
    g0 = NSA_WIDTH + 6 * KV_W
    gate_cols = jnp.pad(w[:, g0:g0 + NSA_HEADS * 3], ((0, 0), (0, LANES - NSA_HEADS * 3)))
    conv_cols = w[:, g0 + NSA_HEADS * 3:]
    w_perm = jnp.concatenate([q_cols, kv_cols, gate_cols, conv_cols], axis=1).astype(jnp.bfloat16)

    q4, kc, vc, ks, vs, kw, vw, gates, conv_n = _in_projection(
        x2, mod3, ln_mix_g[0].reshape(1, D_MODEL), w_perm, conv_w[0].reshape(CONV_K, CONV_WIDTH),
        norm_conv_g[0].reshape(1, CONV_WIDTH), S)

    ncp = S // CMP_STRIDE
    tok = jnp.stack([kc, vc]).reshape(2, B, ncp, CMP_STRIDE, NSA_KV_HEADS, HEAD_DIM)
    tok = tok.transpose(0, 1, 4, 2, 3, 5).reshape(2, B, NSA_KV_HEADS, ncp, CMP_STRIDE * HEAD_DIM)
    cmp_kv = _compress(tok,
                       jnp.stack([cmp_pe_k[0], cmp_pe_v[0]]).reshape(2, 1, CMP_BLK * HEAD_DIM),
                       jnp.stack([cmp_wk1[0], cmp_wv1[0]]), jnp.stack([cmp_wk2[0], cmp_wv2[0]]))

    b_cmp, b_near = _bias_tables(rel_table, S)
    attn4 = _attention(q4, ks, vs, kw, vw, cmp_kv, gates, b_cmp, b_near, _cmp_to_slc(S),
                       norm_attn_g[0][perm].reshape(NSA_REP, 1, LANES), B, S)

    w_out_p = jnp.concatenate([w_out[0][:NSA_WIDTH][perm], w_out[0][NSA_WIDTH:]], axis=0)
    half = PEER_QDIM // 2
    keys = peer_keys[0].reshape(2 * PEER_HEADS, PEER_NKEYS, half)
    lo = jnp.pad(keys, ((0, 0), (0, 0), (0, half)))
    hi = jnp.pad(keys, ((0, 0), (0, 0), (half, 0)))
    is_hi = (jnp.arange(2 * PEER_HEADS) % 2 == 1)[:, None, None]
    keys_x = jnp.where(is_hi, hi, lo).astype(jnp.bfloat16)
    x1, h2, idx_rows, peer_gates = _out_proj_route(
        attn4, conv_n, x2, mod3, w_out_p.astype(jnp.bfloat16), ln_ffn_g[0].reshape(1, D_MODEL),
        peer_wq[0].astype(jnp.bfloat16), keys_x, S)

    idx_tiles = idx_rows.reshape(n // TT, TT * PEER_E)
    w_act = _peer_act(idx_tiles, h2, peer_gates, _pack_table(peer_u[0]))
    out = _peer_out(idx_tiles, w_act, _pack_table(peer_v[0]), x1, mod3,
                    ln_final_g.reshape(1, D_MODEL), S)
    return out.reshape(B, S, D_MODEL)
```
